```python
import math
import jax, jax.numpy as jnp
from jax import lax
import numpy as np

D_MODEL = 4096
BATCH = 1
SEQ = 8192
DEPTH = 4

HEAD_DIM = 128
BLOCK = 128
A_HEADS = D_MODEL // (2 * HEAD_DIM)
A_KV_HEADS = 2
A_WINDOW = 128
B_HEADS = D_MODEL // (2 * HEAD_DIM)
N_BRANCH = 3
CMP_LEN = 32
CMP_STRIDE = 16
CMP_HIDDEN = 256
SEL_BLOCK = 64
SEL_TOPK = 16
B_WINDOW = 512
C_HEADS = D_MODEL // HEAD_DIM
C_PATTERNS = ((128, 1), (512, 4), (2048, 16))
N_EXPERTS = 16
N_GROUPS = 4
EXPERTS_PER_GROUP = N_EXPERTS // N_GROUPS
TOP_K = 2
D_FF = 768
ALPHA = (2 * DEPTH) ** 0.25
BETA = (8 * DEPTH) ** -0.25
LN_EPS = 1e-5
NEG = -1e30

A_Q_COLS = A_HEADS * HEAD_DIM
A_KV_COLS = A_KV_HEADS * HEAD_DIM
B_Q_COLS = B_HEADS * HEAD_DIM
B_KV_COLS = N_BRANCH * 2 * HEAD_DIM
B_GATE_COLS = B_HEADS * N_BRANCH
AB_SPLITS = (A_Q_COLS, A_Q_COLS + A_KV_COLS, A_Q_COLS + 2 * A_KV_COLS,
             A_Q_COLS + 2 * A_KV_COLS + B_Q_COLS,
             A_Q_COLS + 2 * A_KV_COLS + B_Q_COLS + B_KV_COLS)
AB_IN = AB_SPLITS[-1] + B_GATE_COLS
AB_OUT = (A_HEADS + B_HEADS) * HEAD_DIM
C_OUT = C_HEADS * HEAD_DIM
C_IN = 3 * C_OUT
N_EVEN = (DEPTH + 1) // 2
N_ODD = DEPTH // 2

kernel_name = 'hybrid_swa_nsa_dilated_moe_deepnorm'


def layer_norm(x, g, b):
    xf = x.astype(jnp.float32)
    mu = jnp.mean(xf, axis=-1, keepdims=True)
    var = jnp.mean(jnp.square(xf - mu), axis=-1, keepdims=True)
    return ((xf - mu) * lax.rsqrt(var + LN_EPS) * g.astype(jnp.float32) + b.astype(jnp.float32)).astype(x.dtype)


def alibi_slopes(n):
    return jnp.asarray(2.0 ** (-8.0 * np.arange(1, n + 1) / n), dtype=jnp.float32)


def _band_blocks(t, blk, n_prev):
    b, L, h, d = t.shape
    nb = L // blk
    tb = t.reshape(b, nb, blk, h, d)
    tp = jnp.concatenate([jnp.zeros((b, n_prev, blk, h, d), t.dtype), tb], axis=1)
    return jnp.concatenate([tp[:, p:p + nb] for p in range(n_prev + 1)], axis=2)


def banded_attention(q, k, v, window, slopes, dist_scale):
    b, L, hq, d = q.shape
    hkv = k.shape[2]
    g = hq // hkv
    blk = math.gcd(BLOCK, L)
    n_prev = -(-(window - 1) // blk)
    nb = L // blk
    kn = (n_prev + 1) * blk
    qb = q.reshape(b, nb, blk, hkv, g, d)
    kb = _band_blocks(k, blk, n_prev)
    vb = _band_blocks(v, blk, n_prev)
    s = jnp.einsum('bnqhgd,bnkhd->bhgnqk', qb, kb, preferred_element_type=jnp.float32) * (d ** -0.5)
    dist = n_prev * blk + np.arange(blk)[:, None] - np.arange(kn)[None, :]
    kpos = (np.arange(nb)[:, None] - n_prev) * blk + np.arange(kn)[None, :]
    valid = ((dist >= 0) & (dist < window))[None] & (kpos >= 0)[:, None, :]
    bias = -slopes.reshape(hkv, g, 1, 1, 1) * jnp.asarray(dist * dist_scale, jnp.float32)
    s = jnp.where(jnp.asarray(valid), s + bias, NEG)
    lse = jax.nn.logsumexp(s, axis=-1)
    p = jnp.exp(s - lse[..., None])
    o = jnp.einsum('bhgnqk,bnkhd->bnqhgd', p.astype(v.dtype), vb)
    return o.reshape(b, L, hq, d), lse.transpose(0, 3, 4, 1, 2).reshape(b, L, hq)


def window_sink_attention(q, k, v, sink, slopes):
    o, lse = banded_attention(q, k, v, A_WINDOW, slopes, 1)
    keep = jax.nn.sigmoid(lse - sink.astype(jnp.float32))
    return o * keep[..., None].astype(o.dtype)


def nsa_compress(kv, pe, w1, b1, w2, b2):
    b, L, _, d = kv.shape
    n_cmp = (L - CMP_LEN) // CMP_STRIDE + 1
    idx = np.arange(n_cmp)[:, None] * CMP_STRIDE + np.arange(CMP_LEN)[None, :]
    blocks = kv[:, idx] + pe.transpose(1, 0, 2)
    flat = blocks.transpose(0, 1, 3, 2, 4).reshape(b, n_cmp, 2, CMP_LEN * d)
    h = jax.nn.gelu(jnp.einsum('bnci,cih->bnch', flat, w1) + b1)
    return jnp.einsum('bnch,chd->bncd', h, w2) + b2


def nsa_attention(q, kv_cmp, kv_sel, kv_win, gate_logits, pe, w1, b1, w2, b2, slopes):
    b, L, h, d = q.shape
    scale = d ** -0.5
    tpos = np.arange(L)
    kvc = nsa_compress(kv_cmp, pe, w1, b1, w2, b2)
    kc, vc = kvc[:, :, 0], kvc[:, :, 1]
    n_cmp = kc.shape[1]
    cstart = np.arange(n_cmp) * CMP_STRIDE
    cdist = tpos[:, None] - (cstart + CMP_LEN - 1)[None, :]
    cvalid = jnp.asarray(cdist >= 0)
    s = jnp.einsum('blhd,bnd->bhln', q, kc, preferred_element_type=jnp.float32) * scale
    s = s - slopes[:, None, None] * jnp.asarray(np.maximum(cdist, 0), jnp.float32)
    s = jnp.where(cvalid, s, NEG)
    p_cmp = jnp.where(cvalid, jax.nn.softmax(s, axis=-1), 0.0)
    o_cmp = jnp.einsum('bhln,bnd->blhd', p_cmp.astype(vc.dtype), vc)
    n_sel = L // SEL_BLOCK
    sstart = np.arange(n_sel) * SEL_BLOCK
    overlap = (cstart[:, None] < sstart[None, :] + SEL_BLOCK) & (cstart[:, None] + CMP_LEN > sstart[None, :])
    imp = jnp.einsum('bhln,nj->blj', p_cmp, jnp.asarray(overlap, jnp.float32))
    cur = tpos // SEL_BLOCK
    j = np.arange(n_sel)[None, :]
    forced = (j == 0) | (j == cur[:, None]) | (j == cur[:, None] - 1)
    svalid = sstart[None, :] <= tpos[:, None]
    imp = jnp.where(jnp.asarray(forced), jnp.inf, jnp.where(jnp.asarray(svalid), imp, -jnp.inf))
    n_top = min(SEL_TOPK, n_sel)
    _, sel_idx = lax.top_k(imp, n_top)
    ks, vs = kv_sel[:, :, 0], kv_sel[:, :, 1]
    blk = math.gcd(BLOCK, L)
    nb = L // blk
    qm = q.reshape(b, nb, blk, h, d).swapaxes(0, 1)
    im = sel_idx.reshape(b, nb, blk, n_top).swapaxes(0, 1)
    pm = jnp.arange(L, dtype=jnp.int32).reshape(nb, blk)

    def sel_block(args):
        qb, ib, pb = args
        kpos = (ib[..., None] * SEL_BLOCK + jnp.arange(SEL_BLOCK, dtype=jnp.int32)).reshape(b, blk, n_top * SEL_BLOCK)
        kg = jax.vmap(lambda t, i: t[i])(ks, kpos)
        vg = jax.vmap(lambda t, i: t[i])(vs, kpos)
        dist = (pb[:, None] - kpos).astype(jnp.float32)
        sc = jnp.einsum('bqhd,bqkd->bhqk', qb, kg, preferred_element_type=jnp.float32) * scale
        sc = jnp.where((dist >= 0)[:, None], sc - slopes[:, None, None] * dist[:, None], NEG)
        pr = jax.nn.softmax(sc, axis=-1)
        return jnp.einsum('bhqk,bqkd->bqhd', pr.astype(vg.dtype), vg)

    o_sel = lax.map(sel_block, (qm, im, pm)).swapaxes(0, 1).reshape(b, L, h, d)
    o_win, _ = banded_attention(q, kv_win[:, :, 0:1], kv_win[:, :, 1:2], B_WINDOW, slopes, 1)
    g = jax.nn.sigmoid(gate_logits.astype(jnp.float32)).astype(q.dtype)
    return g[..., 0:1] * o_cmp + g[..., 1:2] * o_sel + g[..., 2:3] * o_win


def ab_mixer(x, w_in, w_out, sink, pe, w1, b1, w2, b2, slopes_a, slopes_b):
    b, L, _ = x.shape
    qa, ka, va, qb, kvb, gb = jnp.split(x @ w_in, AB_SPLITS, axis=-1)
    qa = qa.reshape(b, L, A_HEADS, HEAD_DIM)
    ka = ka.reshape(b, L, A_KV_HEADS, HEAD_DIM)
    va = va.reshape(b, L, A_KV_HEADS, HEAD_DIM)
    qb = qb.reshape(b, L, B_HEADS, HEAD_DIM)
    kvb = kvb.reshape(b, L, N_BRANCH, 2, HEAD_DIM)
    gb = gb.reshape(b, L, B_HEADS, N_BRANCH)
    oa = window_sink_attention(qa, ka, va, sink, slopes_a)
    ob = nsa_attention(qb, kvb[:, :, 0], kvb[:, :, 1], kvb[:, :, 2], gb, pe, w1, b1, w2, b2, slopes_b)
    o = jnp.concatenate([oa.reshape(b, L, A_Q_COLS), ob.reshape(b, L, B_Q_COLS)], axis=-1)
    return o @ w_out


def dilated_attention(q, k, v, slopes):
    b, L, h, d = q.shape
    outs, lses = [], []
    for window, dil in C_PATTERNS:
        m = L // dil

        def to_sub(t):
            return t.reshape(b, m, dil, h, d).swapaxes(1, 2).reshape(b * dil, m, h, d)

        o, lse = banded_attention(to_sub(q), to_sub(k), to_sub(v), window // dil + 1, slopes, dil)
        outs.append(o.reshape(b, dil, m, h, d).swapaxes(1, 2).reshape(b, L, h, d))
        lses.append(lse.reshape(b, dil, m, h).swapaxes(1, 2).reshape(b, L, h))
    w = jax.nn.softmax(jnp.stack(lses, axis=0), axis=0)
    return jnp.einsum('pblh,pblhd->blhd', w.astype(q.dtype), jnp.stack(outs, axis=0))


def c_mixer(x, w_in, w_out, slopes):
    b, L, _ = x.shape
    qkv = (x @ w_in).reshape(b, L, 3, C_HEADS, HEAD_DIM)
    o = dilated_attention(qkv[:, :, 0], qkv[:, :, 1], qkv[:, :, 2], slopes)
    return o.reshape(b, L, C_OUT) @ w_out


def moe(x, router_w, router_b, w_gate, w_up, w_down):
    b, L, _ = x.shape
    logits = jnp.einsum('bld,de->ble', x, router_w, preferred_element_type=jnp.float32) + router_b.astype(jnp.float32)
    probs = jax.nn.softmax(logits, axis=-1)
    grouped = probs.reshape(b, L, N_GROUPS, EXPERTS_PER_GROUP)
    group_score = jnp.sum(lax.top_k(grouped, TOP_K)[0], axis=-1)
    g_sel = jnp.argmax(group_score, axis=-1)
    in_group = jnp.einsum('blge,blg->ble', grouped, jax.nn.one_hot(g_sel, N_GROUPS, dtype=jnp.float32))
    top_p, top_i = lax.top_k(in_group, TOP_K)
    top_p = top_p / jnp.sum(top_p, axis=-1, keepdims=True)
    expert = g_sel[..., None] * EXPERTS_PER_GROUP + top_i
    gates = jnp.sum(jax.nn.one_hot(expert, N_EXPERTS, dtype=jnp.float32) * top_p[..., None], axis=-2)
    h = jnp.einsum('bld,edf->blef', x, w_gate)
    u = jnp.einsum('bld,edf->blef', x, w_up)
    act = jax.nn.silu(h) * u * gates[..., None].astype(x.dtype)
    return jnp.einsum('blef,efd->bld', act, w_down)


def setup_inputs(seed: int = 0) -> dict:
    key = jax.random.key(seed)
    ks = jax.random.split(key, 24)
    f32 = jnp.float32

    def nrm(k, shape, scale):
        return jax.random.normal(k, shape, f32) * scale

    d = D_MODEL
    b_kv_scale = np.broadcast_to(np.array([1.0, BETA], np.float32)[None, :, None], (N_BRANCH, 2, HEAD_DIM)).reshape(-1)
    ab_scale = np.concatenate([np.ones(A_Q_COLS + A_KV_COLS), np.full(A_KV_COLS, BETA), np.ones(B_Q_COLS),
                               b_kv_scale, np.ones(B_GATE_COLS)]).astype(np.float32)
    c_scale = np.concatenate([np.ones(2 * C_OUT), np.full(C_OUT, BETA)]).astype(np.float32)
    return {
        'x': nrm(ks[0], (BATCH, SEQ, d), 1.0),
        'ab_w_in': nrm(ks[1], (N_EVEN, d, AB_IN), d ** -0.5) * jnp.asarray(ab_scale),
        'ab_w_out': nrm(ks[2], (N_EVEN, AB_OUT, d), AB_OUT ** -0.5 * BETA),
        'a_sink': nrm(ks[3], (N_EVEN, A_HEADS), 1.0),
        'nsa_cmp_pe': nrm(ks[4], (N_EVEN, 2, CMP_LEN, HEAD_DIM), 0.1),
        'nsa_cmp_w1': nrm(ks[5], (N_EVEN, 2, CMP_LEN * HEAD_DIM, CMP_HIDDEN), (CMP_LEN * HEAD_DIM) ** -0.5),
        'nsa_cmp_b1': nrm(ks[6], (N_EVEN, 2, CMP_HIDDEN), 0.02),
        'nsa_cmp_w2': nrm(ks[7], (N_EVEN, 2, CMP_HIDDEN, HEAD_DIM), CMP_HIDDEN ** -0.5),
        'nsa_cmp_b2': nrm(ks[8], (N_EVEN, 2, HEAD_DIM), 0.02),
        'c_w_in': nrm(ks[9], (N_ODD, d, C_IN), d ** -0.5) * jnp.asarray(c_scale),
        'c_w_out': nrm(ks[10], (N_ODD, C_OUT, d), C_OUT ** -0.5 * BETA),
        'ln_mix_g': 1.0 + nrm(ks[11], (DEPTH, d), 0.02),
        'ln_mix_b': nrm(ks[12], (DEPTH, d), 0.02),
        'ln_ffn_g': 1.0 + nrm(ks[13], (DEPTH, d), 0.02),
        'ln_ffn_b': nrm(ks[14], (DEPTH, d), 0.02),
        'router_w': nrm(ks[15], (d, N_EXPERTS), d ** -0.5),
        'router_b': nrm(ks[16], (N_EXPERTS,), 0.01),
        'moe_w_gate': nrm(ks[17], (DEPTH, N_EXPERTS, d, D_FF), d ** -0.5),
        'moe_w_up': nrm(ks[18], (DEPTH, N_EXPERTS, d, D_FF), d ** -0.5),
        'moe_w_down': nrm(ks[19], (DEPTH, N_EXPERTS, D_FF, d), D_FF ** -0.5 * BETA),
    }


def reference(x, ab_w_in, ab_w_out, a_sink, nsa_cmp_pe, nsa_cmp_w1, nsa_cmp_b1, nsa_cmp_w2, nsa_cmp_b2,
              c_w_in, c_w_out, ln_mix_g, ln_mix_b, ln_ffn_g, ln_ffn_b, router_w, router_b,
              moe_w_gate, moe_w_up, moe_w_down):
    slopes_a = alibi_slopes(A_HEADS)
    slopes_b = alibi_slopes(B_HEADS)
    slopes_c = alibi_slopes(C_HEADS)
    for layer in range(DEPTH):
        i = layer // 2
        if layer % 2 == 0:
            mix = ab_mixer(x, ab_w_in[i], ab_w_out[i], a_sink[i], nsa_cmp_pe[i], nsa_cmp_w1[i], nsa_cmp_b1[i],
                           nsa_cmp_w2[i], nsa_cmp_b2[i], slopes_a, slopes_b)
        else:
            mix = c_mixer(x, c_w_in[i], c_w_out[i], slopes_c)
        x = layer_norm(ALPHA * x + mix, ln_mix_g[layer], ln_mix_b[layer])
        ffn = moe(x, router_w, router_b, moe_w_gate[layer], moe_w_up[layer], moe_w_down[layer])
        x = layer_norm(ALPHA * x + ffn, ln_ffn_g[layer], ln_ffn_b[layer])
    return x
```

```python
import functools
import math

import numpy as np
import jax
import jax.numpy as jnp
from jax import lax
from jax.experimental import pallas as pl
from jax.experimental.pallas import tpu as pltpu

HEAD_DIM = 128
BLK = 128
A_KV_HEADS = 2
A_WINDOW = 128
N_BRANCH = 3
CMP_LEN = 32
CMP_STRIDE = 16
CMP_HIDDEN = 256
SEL_BLOCK = 64
SEL_TOPK = 16
B_WINDOW = 512
C_PATTERNS = ((128, 1), (512, 4), (2048, 16))
N_EXPERTS = 16
N_GROUPS = 4
EXPERTS_PER_GROUP = N_EXPERTS // N_GROUPS
TOP_K = 2
LN_EPS = 1e-5
NEG = -1e30
SCALE = HEAD_DIM ** -0.5

LANES = 128
VMEM_BUDGET = 56 * 1024 * 1024

F32 = jnp.float32
BF16 = jnp.bfloat16
NT_DIMS = (((1,), (1,)), ((), ()))


def _params(n_grid, vmem_bytes):
    return pltpu.CompilerParams(
        dimension_semantics=("arbitrary",) * n_grid,
        vmem_limit_bytes=int(min(VMEM_BUDGET, max(vmem_bytes, 16 * 1024 * 1024))),
    )


def _alibi_slopes(n):
    return jnp.asarray(2.0 ** (-8.0 * np.arange(1, n + 1) / n), dtype=F32)


def _layer_norm_rows(z, g, b):
    mu = jnp.mean(z, axis=-1, keepdims=True)
    zc = z - mu
    var = jnp.mean(zc * zc, axis=-1, keepdims=True)
    return zc * lax.rsqrt(var + LN_EPS) * g + b


def _proj_kernel(x_ref, w_ref, o_ref):
    acc = jnp.dot(x_ref[...], w_ref[...], preferred_element_type=F32)
    for c in range(o_ref.shape[0]):
        o_ref[c] = acc[:, c * LANES:(c + 1) * LANES].astype(o_ref.dtype)


def proj_matmul(x, w, out_dtype):
    m, k = x.shape
    n = w.shape[1]
    tm = min(1024, m)
    tn = min(512, n)
    osz = jnp.dtype(out_dtype).itemsize
    vmem = 2 * (tm * k * 2 + k * tn * 2 + tm * tn * osz) + 2 * tm * tn * 4
    return pl.pallas_call(
        _proj_kernel,
        grid=(m // tm, n // tn),
        in_specs=[pl.BlockSpec((tm, k), lambda i, j: (i, 0)),
                  pl.BlockSpec((k, tn), lambda i, j: (0, j))],
        out_specs=pl.BlockSpec((tn // LANES, tm, LANES), lambda i, j: (j, i, 0)),
        out_shape=jax.ShapeDtypeStruct((n // LANES, m, LANES), out_dtype),
        compiler_params=_params(2, vmem + (8 << 20)),
        name="proj_matmul",
    )(x, w)


def _banded_kernel(slopes_ref, sinks_ref, q_ref, kc_ref, kp_ref, vc_ref, vp_ref, *rest,
                   G, R, PR, window, has_sink, gate_col):
    if gate_col is not None:
        g_ref, oacc_ref, o_ref, kbuf, vbuf = rest
    else:
        o_ref, kbuf, vbuf = rest
    kv = pl.program_id(0)
    i = pl.program_id(1)
    kn = PR + BLK
    kbuf[0:PR, :] = kp_ref[0]
    kbuf[PR:PR + R, :] = kc_ref[0]
    vbuf[0:PR, :] = vp_ref[0]
    vbuf[PR:PR + R, :] = vc_ref[0]
    qi = lax.broadcasted_iota(jnp.int32, (BLK, kn), 0)
    kc = lax.broadcasted_iota(jnp.int32, (BLK, kn), 1)
    dist = PR + qi - kc
    static_valid = (dist >= 0) & (dist < window)
    negd = -dist.astype(F32)

    def unit(j, carry):
        row0 = pl.multiple_of(j * BLK, BLK)
        first_key = PR - (i * R + j * BLK)
        valid = static_valid & (kc >= first_key)
        k = kbuf[pl.ds(row0, kn), :]
        v = vbuf[pl.ds(row0, kn), :]
        q = q_ref[:, pl.ds(row0, BLK), :].reshape(G * BLK, HEAD_DIM)
        s_all = lax.dot_general(q, k, NT_DIMS, preferred_element_type=F32)
        ps, ls = [], []
        for g in range(G):
            h = kv * G + g
            s = s_all[g * BLK:(g + 1) * BLK] * SCALE + slopes_ref[h] * negd
            s = jnp.where(valid, s, NEG)
            m = jnp.max(s, axis=-1, keepdims=True)
            p = jnp.exp(s - m)
            l = jnp.sum(p, axis=-1, keepdims=True)
            if has_sink:
                l = l + jnp.exp(sinks_ref[h] - m)
            ps.append(p.astype(BF16))
            ls.append(l)
        acc = jnp.dot(jnp.concatenate(ps, axis=0), v, preferred_element_type=F32)
        for g in range(G):
            o = acc[g * BLK:(g + 1) * BLK] / ls[g]
            if gate_col is not None:
                gl = g_ref[0, pl.ds(row0, BLK), :].astype(F32)
                c = g * N_BRANCH + gate_col
                o = oacc_ref[g, pl.ds(row0, BLK), :] + jax.nn.sigmoid(gl[:, c:c + 1]) * o
            o_ref[g, pl.ds(row0, BLK), :] = o.astype(o_ref.dtype)
        return carry

    lax.fori_loop(0, R // BLK, unit, 0)


def banded_attention(proj, slopes, sinks, *, q_base, k_blk, v_blk, n_kv, G, window,
                     has_sink=False, gate_blk=None, gate_col=None, oacc=None):
    L = proj.shape[1]
    n_prev = -(-(window - 1) // BLK)
    PR = n_prev * BLK
    R = max(PR, min(L, 8192 // G))
    assert L % R == 0 and R % PR == 0 and q_base % G == 0
    qb0 = q_base // G
    rp = R // PR
    in_specs = [
        pl.BlockSpec((G, R, LANES), lambda kv, i, *_: (qb0 + kv, i, 0)),
        pl.BlockSpec((1, R, LANES), lambda kv, i, *_: (k_blk + kv, i, 0)),
        pl.BlockSpec((1, PR, LANES), lambda kv, i, *_: (k_blk + kv, jnp.maximum(i * rp - 1, 0), 0)),
        pl.BlockSpec((1, R, LANES), lambda kv, i, *_: (v_blk + kv, i, 0)),
        pl.BlockSpec((1, PR, LANES), lambda kv, i, *_: (v_blk + kv, jnp.maximum(i * rp - 1, 0), 0)),
    ]
    args = [proj, proj, proj, proj, proj]
    vmem = 2 * (G * R * LANES * 2 * 2 + 4 * (R + PR) * LANES * 2) + 2 * (R + PR) * LANES * 2
    if gate_col is not None:
        assert n_kv == 1
        in_specs += [pl.BlockSpec((1, R, LANES), lambda kv, i, *_: (gate_blk, i, 0)),
                     pl.BlockSpec((G, R, LANES), lambda kv, i, *_: (kv, i, 0))]
        args += [proj, oacc]
        vmem += 2 * (R * LANES * 2 + G * R * LANES * 4)
    kern = functools.partial(_banded_kernel, G=G, R=R, PR=PR, window=window,
                             has_sink=has_sink, gate_col=gate_col)
    return pl.pallas_call(
        kern,
        grid_spec=pltpu.PrefetchScalarGridSpec(
            num_scalar_prefetch=2,
            grid=(n_kv, L // R),
            in_specs=in_specs,
            out_specs=pl.BlockSpec((G, R, LANES), lambda kv, i, *_: (kv, i, 0)),
            scratch_shapes=[pltpu.VMEM((PR + R, LANES), BF16), pltpu.VMEM((PR + R, LANES), BF16)],
        ),
        out_shape=jax.ShapeDtypeStruct((n_kv * G, L, LANES), BF16),
        compiler_params=_params(2, vmem + (16 << 20)),
        name="banded_attention",
    )(slopes, sinks, *args)


def _compress_kernel(g_ref, pe_ref, w1_ref, b1_ref, w2_ref, b2_ref, o_ref):
    ng = g_ref.shape[1]
    half = w1_ref.shape[1] // 2
    g = g_ref[0]
    y0 = jnp.dot(g, w1_ref[0, 0:half, :], preferred_element_type=F32)
    y1 = jnp.dot(g, w1_ref[0, half:, :], preferred_element_type=F32)
    y1 = pltpu.roll(y1, ng - 1, 0)
    pe = jnp.broadcast_to(pe_ref[0], (8, pe_ref.shape[2]))
    ype = jnp.dot(pe, w1_ref[0], preferred_element_type=F32)[0:1]
    h = jax.nn.gelu(y0 + y1 + ype + b1_ref[0])
    out = jnp.dot(h.astype(BF16), w2_ref[0], preferred_element_type=F32) + b2_ref[0]
    o_ref[0] = out.astype(o_ref.dtype)


def nsa_compress(groups, pe, w1, b1, w2, b2):
    _, ng, gw = groups.shape
    hid = w1.shape[2]
    vmem = 2 * (ng * gw * 2 + 2 * gw * hid * 2 + 2 * gw * 2) + 6 * ng * hid * 4
    return pl.pallas_call(
        _compress_kernel,
        grid=(2,),
        in_specs=[pl.BlockSpec((1, ng, gw), lambda c: (c, 0, 0)),
                  pl.BlockSpec((1, 1, 2 * gw), lambda c: (c, 0, 0)),
                  pl.BlockSpec((1, 2 * gw, hid), lambda c: (c, 0, 0)),
                  pl.BlockSpec((1, 1, hid), lambda c: (c, 0, 0)),
                  pl.BlockSpec((1, hid, HEAD_DIM), lambda c: (c, 0, 0)),
                  pl.BlockSpec((1, 1, HEAD_DIM), lambda c: (c, 0, 0))],
        out_specs=pl.BlockSpec((1, ng, HEAD_DIM), lambda c: (c, 0, 0)),
        out_shape=jax.ShapeDtypeStruct((2, ng, HEAD_DIM), BF16),
        compiler_params=_params(1, vmem + (8 << 20)),
        name="nsa_compress",
    )(groups, pe, w1, b1, w2, b2)


def _cmp_kernel(slopes_ref, q_ref, kv_ref, g_ref, ov_ref, o_ref, sel_ref, *, H, TQ, n_cmp, n_sel):
    qb = pl.program_id(0)
    ncp = kv_ref.shape[1]
    nsp = sel_ref.shape[1]
    kc = kv_ref[0]
    vc = kv_ref[1]
    t = qb * TQ + lax.broadcasted_iota(jnp.int32, (TQ, 1), 0)
    n = lax.broadcasted_iota(jnp.int32, (1, ncp), 1)
    cdist = t - (n * CMP_STRIDE + (CMP_LEN - 1))
    valid = (cdist >= 0) & (n < n_cmp)
    negd = -jnp.maximum(cdist, 0).astype(F32)
    gl = g_ref[0].astype(F32)
    psum = jnp.zeros((TQ, ncp), F32)
    for h in range(H):
        s = lax.dot_general(q_ref[h], kc, NT_DIMS, preferred_element_type=F32) * SCALE
        s = jnp.where(valid, s + slopes_ref[h] * negd, NEG)
        m = jnp.max(s, axis=-1, keepdims=True)
        e = jnp.exp(s - m)
        l = jnp.sum(e, axis=-1, keepdims=True)
        p = jnp.where(valid, e / l, 0.0)
        psum = psum + p
        o = jnp.dot(p.astype(BF16), vc, preferred_element_type=F32)
        c = h * N_BRANCH
        o_ref[h] = jax.nn.sigmoid(gl[:, c:c + 1]) * o
    p_hi = psum.astype(BF16)
    p_lo = (psum - p_hi.astype(F32)).astype(BF16)
    imp = (jnp.dot(p_hi, ov_ref[...], preferred_element_type=F32)
           + jnp.dot(p_lo, ov_ref[...], preferred_element_type=F32))
    j = lax.broadcasted_iota(jnp.int32, (1, nsp), 1)
    jf = j.astype(F32)
    cur = t // SEL_BLOCK
    forced = (j == 0) | (j == cur) | (j == cur - 1)
    svalid = j * SEL_BLOCK <= t
    work = jnp.where(forced, 1e30, jnp.where(svalid, imp, -1.0))
    work = jnp.where(j < n_sel, work, -2.0)
    sel = jnp.zeros((TQ, nsp), F32)
    for _ in range(min(SEL_TOPK, n_sel)):
        mx = jnp.max(work, axis=-1, keepdims=True)
        idx = jnp.min(jnp.where(work == mx, jf, 1e9), axis=-1, keepdims=True)
        hit = jf == idx
        sel = jnp.where(hit, 1.0, sel)
        work = jnp.where(hit, -2.0, work)
    sel_ref[...] = sel


def nsa_cmp_attention(proj, kvc, slopes, overlap, *, q_base, H, gate_blk, n_cmp, n_sel):
    L = proj.shape[1]
    ncp = kvc.shape[1]
    nsp = overlap.shape[1]
    TQ = min(256, L)
    kern = functools.partial(_cmp_kernel, H=H, TQ=TQ, n_cmp=n_cmp, n_sel=n_sel)
    vmem = 2 * (H * TQ * LANES * 2 + 2 * ncp * LANES * 2 + TQ * LANES * 2 + ncp * nsp * 2
                + H * TQ * LANES * 4 + TQ * nsp * 4) + 8 * TQ * ncp * 4
    return pl.pallas_call(
        kern,
        grid_spec=pltpu.PrefetchScalarGridSpec(
            num_scalar_prefetch=1,
            grid=(L // TQ,),
            in_specs=[pl.BlockSpec((H, TQ, LANES), lambda qb, *_: (q_base // H, qb, 0)),
                      pl.BlockSpec((2, ncp, LANES), lambda qb, *_: (0, 0, 0)),
                      pl.BlockSpec((1, TQ, LANES), lambda qb, *_: (gate_blk, qb, 0)),
                      pl.BlockSpec((ncp, nsp), lambda qb, *_: (0, 0))],
            out_specs=[pl.BlockSpec((H, TQ, LANES), lambda qb, *_: (0, qb, 0)),
                       pl.BlockSpec((TQ, nsp), lambda qb, *_: (qb, 0))],
        ),
        out_shape=[jax.ShapeDtypeStruct((H, L, LANES), F32),
                   jax.ShapeDtypeStruct((L, nsp), F32)],
        compiler_params=_params(1, vmem + (8 << 20)),
        name="nsa_cmp_attention",
    )(slopes, proj, kvc, proj, overlap)


def _sel_kernel(slopes_ref, q_ref, k_ref, v_ref, sel_ref, g_ref, oacc_ref, o_ref, m_s, l_s, acc_s,
                *, H, TQ, TK):
    qb = pl.program_id(0)
    kb = pl.program_id(1)
    nkb = pl.num_programs(1)
    nsp = sel_ref.shape[1]

    @pl.when(kb == 0)
    def _():
        m_s[...] = jnp.full(m_s.shape, NEG, F32)
        l_s[...] = jnp.zeros(l_s.shape, F32)
        acc_s[...] = jnp.zeros(acc_s.shape, F32)

    @pl.when(kb * TK <= qb * TQ + (TQ - 1))
    def _():
        k = k_ref[0]
        v = v_ref[0]
        t = qb * TQ + lax.broadcasted_iota(jnp.int32, (TQ, 1), 0)
        kk = kb * TK + lax.broadcasted_iota(jnp.int32, (1, TK), 1)
        jrow = lax.broadcasted_iota(jnp.int32, (nsp, 1), 0)
        expand = jnp.where(jrow == kk // SEL_BLOCK, 1.0, 0.0).astype(BF16)
        chosen = jnp.dot(sel_ref[...].astype(BF16), expand, preferred_element_type=F32)
        mask = (chosen > 0.5) & (kk <= t)
        rel = (kk - t).astype(F32)
        for h in range(H):
            s = lax.dot_general(q_ref[h], k, NT_DIMS, preferred_element_type=F32) * SCALE
            s = jnp.where(mask, s + slopes_ref[h] * rel, NEG)
            m_old = m_s[h]
            m_new = jnp.maximum(m_old, jnp.max(s, axis=-1, keepdims=True))
            alpha = jnp.exp(m_old - m_new)
            p = jnp.where(mask, jnp.exp(s - m_new), 0.0)
            l_s[h] = alpha * l_s[h] + jnp.sum(p, axis=-1, keepdims=True)
            acc_s[h] = alpha * acc_s[h] + jnp.dot(p.astype(BF16), v, preferred_element_type=F32)
            m_s[h] = m_new

    @pl.when(kb == nkb - 1)
    def _():
        gl = g_ref[0].astype(F32)
        for h in range(H):
            c = h * N_BRANCH + 1
            o_ref[h] = oacc_ref[h] + jax.nn.sigmoid(gl[:, c:c + 1]) * (acc_s[h] / l_s[h])


def nsa_sel_attention(proj, sel, oacc, slopes, *, q_base, H, k_blk, v_blk, gate_blk):
    L = proj.shape[1]
    nsp = sel.shape[1]
    TQ = min(128, L)
    TK = min(512, L)

    def kv_idx(blk):
        def idx(qb, kb, *_):
            return (blk, jnp.minimum(kb, (qb * TQ + TQ - 1) // TK), 0)
        return idx

    kern = functools.partial(_sel_kernel, H=H, TQ=TQ, TK=TK)
    vmem = (2 * (H * TQ * LANES * 2 + 2 * TK * LANES * 2 + TQ * nsp * 4 + TQ * LANES * 2
                 + 2 * H * TQ * LANES * 4) + 3 * H * TQ * LANES * 4 + 8 * TQ * TK * 4)
    return pl.pallas_call(
        kern,
        grid_spec=pltpu.PrefetchScalarGridSpec(
            num_scalar_prefetch=1,
            grid=(L // TQ, L // TK),
            in_specs=[pl.BlockSpec((H, TQ, LANES), lambda qb, kb, *_: (q_base // H, qb, 0)),
                      pl.BlockSpec((1, TK, LANES), kv_idx(k_blk)),
                      pl.BlockSpec((1, TK, LANES), kv_idx(v_blk)),
                      pl.BlockSpec((TQ, nsp), lambda qb, kb, *_: (qb, 0)),
                      pl.BlockSpec((1, TQ, LANES), lambda qb, kb, *_: (gate_blk, qb, 0)),
                      pl.BlockSpec((H, TQ, LANES), lambda qb, kb, *_: (0, qb, 0))],
            out_specs=pl.BlockSpec((H, TQ, LANES), lambda qb, kb, *_: (0, qb, 0)),
            scratch_shapes=[pltpu.VMEM((H, TQ, 1), F32), pltpu.VMEM((H, TQ, 1), F32),
                            pltpu.VMEM((H, TQ, LANES), F32)],
        ),
        out_shape=jax.ShapeDtypeStruct((H, L, LANES), F32),
        compiler_params=_params(2, vmem + (8 << 20)),
        name="nsa_sel_attention",
    )(slopes, proj, proj, proj, sel, proj, oacc)


def _dilated_kernel(slopes_ref, q_ref, kc_ref, kp_ref, vc_ref, vp_ref, o_ref, kbuf, vbuf, o_s, l_s, *, CH):
    h = pl.program_id(0)
    i = pl.program_id(1)
    slope = slopes_ref[h]
    kbuf[0:CH, :] = kp_ref[0]
    kbuf[CH:2 * CH, :] = kc_ref[0]
    vbuf[0:CH, :] = vp_ref[0]
    vbuf[CH:2 * CH, :] = vc_ref[0]
    kn = 2 * BLK
    qi = lax.broadcasted_iota(jnp.int32, (BLK, kn), 0)
    kc = lax.broadcasted_iota(jnp.int32, (BLK, kn), 1)
    dist = BLK + qi - kc
    static_valid = (dist >= 0) & (dist <= BLK)
    negd = -dist.astype(F32)

    def unit(p, dil, start, is_first):
        def rows(s0, n):
            return pl.ds(s0, n) if dil == 1 else pl.ds(s0, n, stride=dil)
        q = q_ref[0, rows(start, BLK), :].astype(BF16)
        k0 = CH + start - BLK * dil
        k = kbuf[rows(k0, kn), :].astype(BF16)
        v = vbuf[rows(k0, kn), :].astype(BF16)
        first_key = jnp.where(is_first, BLK, 0)
        valid = static_valid & (kc >= first_key)
        s = lax.dot_general(q, k, NT_DIMS, preferred_element_type=F32) * SCALE + (slope * dil) * negd
        s = jnp.where(valid, s, NEG)
        m = jnp.max(s, axis=-1, keepdims=True)
        e = jnp.exp(s - m)
        l = jnp.sum(e, axis=-1, keepdims=True)
        o = jnp.dot(e.astype(BF16), v, preferred_element_type=F32) / l
        o_s[p, rows(start, BLK), :] = o
        l_s[p, rows(start, BLK), :] = jnp.broadcast_to(m + jnp.log(l), (BLK, LANES))

    for p, (window, dil) in enumerate(C_PATTERNS):
        assert window // dil == BLK
        nj = CH // (BLK * dil)

        def body(u, carry, p=p, dil=dil, nj=nj):
            j = u // dil
            r = u % dil
            start = j * (BLK * dil) + r
            if dil == 1:
                start = pl.multiple_of(start, BLK)
            unit(p, dil, start, (i == 0) & (j == 0))
            return carry

        lax.fori_loop(0, nj * dil, body, 0)

    lses = [l_s[p] for p in range(len(C_PATTERNS))]
    mx = functools.reduce(jnp.maximum, lses)
    ws = [jnp.exp(x - mx) for x in lses]
    num = functools.reduce(lambda a, b: a + b, [w * o_s[p] for p, w in enumerate(ws)])
    den = functools.reduce(lambda a, b: a + b, ws)
    o_ref[0] = (num / den).astype(o_ref.dtype)


def dilated_attention(qkv, slopes, *, H):
    L = qkv.shape[1]
    CH = max(d for _, d in C_PATTERNS) * BLK
    assert L % CH == 0
    kern = functools.partial(_dilated_kernel, CH=CH)
    npat = len(C_PATTERNS)
    vmem = 2 * (5 * CH * LANES * 4 + CH * LANES * 2) + (4 + 2 * npat) * CH * LANES * 4
    return pl.pallas_call(
        kern,
        grid_spec=pltpu.PrefetchScalarGridSpec(
            num_scalar_prefetch=1,
            grid=(H, L // CH),
            in_specs=[pl.BlockSpec((1, CH, LANES), lambda h, i, *_: (h, i, 0)),
                      pl.BlockSpec((1, CH, LANES), lambda h, i, *_: (H + h, i, 0)),
                      pl.BlockSpec((1, CH, LANES), lambda h, i, *_: (H + h, jnp.maximum(i - 1, 0), 0)),
                      pl.BlockSpec((1, CH, LANES), lambda h, i, *_: (2 * H + h, i, 0)),
                      pl.BlockSpec((1, CH, LANES), lambda h, i, *_: (2 * H + h, jnp.maximum(i - 1, 0), 0))],
            out_specs=pl.BlockSpec((1, CH, LANES), lambda h, i, *_: (h, i, 0)),
            scratch_shapes=[pltpu.VMEM((2 * CH, LANES), F32), pltpu.VMEM((2 * CH, LANES), F32),
                            pltpu.VMEM((npat, CH, LANES), F32), pltpu.VMEM((npat, CH, LANES), F32)],
        ),
        out_shape=jax.ShapeDtypeStruct((H, L, LANES), BF16),
        compiler_params=_params(2, vmem + (16 << 20)),
        name="dilated_attention",
    )(slopes, qkv, qkv, qkv, qkv, qkv)


def _outproj_ln_kernel(o_ref, w_ref, x_ref, g_ref, b_ref, xo_ref, xb_ref, acc, *, alpha):
    k = pl.program_id(1)

    @pl.when(k == 0)
    def _():
        acc[...] = jnp.zeros(acc.shape, F32)

    lhs = jnp.concatenate([o_ref[c] for c in range(o_ref.shape[0])], axis=1)
    acc[...] += jnp.dot(lhs, w_ref[...], preferred_element_type=F32)

    @pl.when(k == pl.num_programs(1) - 1)
    def _():
        y = _layer_norm_rows(alpha * x_ref[...] + acc[...], g_ref[...], b_ref[...])
        xo_ref[...] = y
        xb_ref[...] = y.astype(BF16)


def outproj_ln(o_heads, w, x, g, b, alpha):
    nh, m, _ = o_heads.shape
    d = w.shape[1]
    tm = min(256, m)
    hb = min(4, nh)
    tk = hb * LANES
    kern = functools.partial(_outproj_ln_kernel, alpha=alpha)
    vmem = 2 * (tm * tk * 2 + tk * d * 2 + tm * d * 4 + tm * d * 4 + tm * d * 2) + 3 * tm * d * 4
    return pl.pallas_call(
        kern,
        grid=(m // tm, nh // hb),
        in_specs=[pl.BlockSpec((hb, tm, LANES), lambda i, k: (k, i, 0)),
                  pl.BlockSpec((tk, d), lambda i, k: (k, 0)),
                  pl.BlockSpec((tm, d), lambda i, k: (i, 0)),
                  pl.BlockSpec((1, d), lambda i, k: (0, 0)),
                  pl.BlockSpec((1, d), lambda i, k: (0, 0))],
        out_specs=[pl.BlockSpec((tm, d), lambda i, k: (i, 0)),
                   pl.BlockSpec((tm, d), lambda i, k: (i, 0))],
        out_shape=[jax.ShapeDtypeStruct((m, d), F32), jax.ShapeDtypeStruct((m, d), BF16)],
        scratch_shapes=[pltpu.VMEM((tm, d), F32)],
        compiler_params=_params(2, vmem + (8 << 20)),
        name="outproj_ln",
    )(o_heads, w, x, g, b)


def _router_kernel(x_ref, wt_ref, b_ref, eid_ref, gate_ref):
    logits = lax.dot_general(wt_ref[...], x_ref[...], NT_DIMS, preferred_element_type=F32,
                             precision=lax.Precision.HIGHEST) + b_ref[...]
    mx = jnp.max(logits, axis=0, keepdims=True)
    e = jnp.exp(logits - mx)
    probs = e / jnp.sum(e, axis=0, keepdims=True)
    rows = [probs[i:i + 1, :] for i in range(N_EXPERTS)]
    epg = EXPERTS_PER_GROUP
    best = None
    gsel = None
    for gi in range(N_GROUPS):
        grp = rows[gi * epg:(gi + 1) * epg]
        score = None
        for a in range(epg):
            for c in range(a + 1, epg):
                pair = grp[a] + grp[c]
                score = pair if score is None else jnp.maximum(score, pair)
        if best is None:
            best, gsel = score, jnp.zeros(score.shape, jnp.int32)
        else:
            better = score > best
            gsel = jnp.where(better, gi, gsel)
            best = jnp.maximum(best, score)
    ing = []
    for kk in range(epg):
        val = rows[kk]
        for gi in range(1, N_GROUPS):
            val = jnp.where(gsel == gi, rows[gi * epg + kk], val)
        ing.append(val)

    def argmax_first(vals):
        bv, bi = vals[0], jnp.zeros(vals[0].shape, jnp.int32)
        for kk in range(1, len(vals)):
            better = vals[kk] > bv
            bi = jnp.where(better, kk, bi)
            bv = jnp.maximum(bv, vals[kk])
        return bv, bi

    v1, i1 = argmax_first(ing)
    rest = [jnp.where(i1 == kk, -1.0, ing[kk]) for kk in range(epg)]
    v2, i2 = argmax_first(rest)
    tot = v1 + v2
    eid_ref[0:1, :] = gsel * epg + i1
    eid_ref[1:2, :] = gsel * epg + i2
    gate_ref[0:1, :] = v1 / tot
    gate_ref[1:2, :] = v2 / tot


def moe_router(x, wt, b):
    m, d = x.shape
    tr = min(512, m)
    vmem = 2 * (tr * d * 4 + N_EXPERTS * d * 4) + 8 * tr * d * 2
    return pl.pallas_call(
        _router_kernel,
        grid=(m // tr,),
        in_specs=[pl.BlockSpec((tr, d), lambda i: (i, 0)),
                  pl.BlockSpec((N_EXPERTS, d), lambda i: (0, 0)),
                  pl.BlockSpec((N_EXPERTS, 1), lambda i: (0, 0))],
        out_specs=[pl.BlockSpec((TOP_K, tr), lambda i: (0, i)),
                   pl.BlockSpec((TOP_K, tr), lambda i: (0, i))],
        out_shape=[jax.ShapeDtypeStruct((TOP_K, m), jnp.int32), jax.ShapeDtypeStruct((TOP_K, m), F32)],
        compiler_params=_params(1, vmem + (8 << 20)),
        name="moe_router",
    )(x, wt, b)


def _row_copy(src_hbm, row, dst, slot, sem):
    return pltpu.make_async_copy(src_hbm.at[pl.ds(row, 1)], dst.at[pl.ds(slot, 1)], sem)


def _moe_up_kernel(te_ref, nt_ref, idx_ref, x_hbm, wg_ref, wu_ref, act_ref, xbuf, sem, *, TM):
    t = pl.program_id(0)

    @pl.when(t < nt_ref[0])
    def _():
        def start(r, c):
            _row_copy(x_hbm, idx_ref[0, 0, r], xbuf, r, sem).start()
            return c
        lax.fori_loop(0, TM, start, 0)

        def wait(r, c):
            _row_copy(x_hbm, 0, xbuf, r, sem).wait()
            return c
        lax.fori_loop(0, TM, wait, 0)
        xb = xbuf[...].astype(BF16)
        hg = jnp.dot(xb, wg_ref[0], preferred_element_type=F32)
        hu = jnp.dot(xb, wu_ref[0], preferred_element_type=F32)
        act_ref[...] = (jax.nn.silu(hg) * hu).astype(act_ref.dtype)

    @pl.when(t >= nt_ref[0])
    def _():
        act_ref[...] = jnp.zeros(act_ref.shape, act_ref.dtype)


def moe_up(x, w_gate, w_up, tile_expert, n_tiles, row_token, TM):
    d = x.shape[1]
    f = w_gate.shape[2]
    tmax = tile_expert.shape[0]
    kern = functools.partial(_moe_up_kernel, TM=TM)
    vmem = 2 * (2 * d * f * 2 + TM * f * 2) + TM * d * 4 + TM * d * 2 + 4 * TM * f * 4
    return pl.pallas_call(
        kern,
        grid_spec=pltpu.PrefetchScalarGridSpec(
            num_scalar_prefetch=2,
            grid=(tmax,),
            in_specs=[pl.BlockSpec((1, 1, TM), lambda t, te, nt: (t, 0, 0), memory_space=pltpu.SMEM),
                      pl.BlockSpec(memory_space=pl.ANY),
                      pl.BlockSpec((1, d, f), lambda t, te, nt: (te[t], 0, 0)),
                      pl.BlockSpec((1, d, f), lambda t, te, nt: (te[t], 0, 0))],
            out_specs=pl.BlockSpec((TM, f), lambda t, te, nt: (t, 0)),
            scratch_shapes=[pltpu.VMEM((TM, d), F32), pltpu.SemaphoreType.DMA(())],
        ),
        out_shape=jax.ShapeDtypeStruct((tmax * TM, f), BF16),
        compiler_params=_params(1, vmem + (8 << 20)),
        name="moe_up",
    )(tile_expert, n_tiles, row_token, x, w_gate, w_up)


def _moe_down_kernel(te_ref, nt_ref, act_ref, wd_ref, y_ref):
    t = pl.program_id(0)

    @pl.when(t < nt_ref[0])
    def _():
        y_ref[...] = jnp.dot(act_ref[...], wd_ref[0], preferred_element_type=F32)

    @pl.when(t >= nt_ref[0])
    def _():
        y_ref[...] = jnp.zeros(y_ref.shape, y_ref.dtype)


def moe_down(act, w_down, tile_expert, n_tiles, TM):
    f = act.shape[1]
    d = w_down.shape[2]
    tmax = tile_expert.shape[0]
    vmem = 2 * (TM * f * 2 + f * d * 2 + TM * d * 4) + TM * d * 4
    return pl.pallas_call(
        _moe_down_kernel,
        grid_spec=pltpu.PrefetchScalarGridSpec(
            num_scalar_prefetch=2,
            grid=(tmax,),
            in_specs=[pl.BlockSpec((TM, f), lambda t, te, nt: (t, 0)),
                      pl.BlockSpec((1, f, d), lambda t, te, nt: (te[t], 0, 0))],
            out_specs=pl.BlockSpec((TM, d), lambda t, te, nt: (t, 0)),
        ),
        out_shape=jax.ShapeDtypeStruct((tmax * TM, d), F32),
        compiler_params=_params(1, vmem + (8 << 20)),
        name="moe_down",
    )(tile_expert, n_tiles, act, w_down)


def _combine_ln_kernel(pos_ref, y_hbm, gate_ref, x_ref, g_ref, b_ref, xo_ref, xb_ref, ybuf, sem, *, TC, alpha):
    def start(r, c):
        _row_copy(y_hbm, pos_ref[0, 0, r], ybuf, r, sem).start()
        return c
    lax.fori_loop(0, TOP_K * TC, start, 0)

    def wait(r, c):
        _row_copy(y_hbm, 0, ybuf, r, sem).wait()
        return c
    lax.fori_loop(0, TOP_K * TC, wait, 0)
    gate = gate_ref[...]
    ffn = gate[:, 0:1] * ybuf[0:TC, :]
    for kk in range(1, TOP_K):
        ffn = ffn + gate[:, kk:kk + 1] * ybuf[kk * TC:(kk + 1) * TC, :]
    y = _layer_norm_rows(alpha * x_ref[...] + ffn, g_ref[...], b_ref[...])
    xo_ref[...] = y
    xb_ref[...] = y.astype(BF16)


def moe_combine_ln(y, pos, gates, x, g, b, alpha, TC):
    m, d = x.shape
    kern = functools.partial(_combine_ln_kernel, TC=TC, alpha=alpha)
    vmem = 2 * (TC * LANES * 4 + 2 * TC * d * 4 + TC * d * 2) + TOP_K * TC * d * 4 + 3 * TC * d * 4
    return pl.pallas_call(
        kern,
        grid=(m // TC,),
        in_specs=[pl.BlockSpec((1, 1, TOP_K * TC), lambda i: (i, 0, 0), memory_space=pltpu.SMEM),
                  pl.BlockSpec(memory_space=pl.ANY),
                  pl.BlockSpec((TC, TOP_K), lambda i: (i, 0)),
                  pl.BlockSpec((TC, d), lambda i: (i, 0)),
                  pl.BlockSpec((1, d), lambda i: (0, 0)),
                  pl.BlockSpec((1, d), lambda i: (0, 0))],
        out_specs=[pl.BlockSpec((TC, d), lambda i: (i, 0)),
                   pl.BlockSpec((TC, d), lambda i: (i, 0))],
        out_shape=[jax.ShapeDtypeStruct((m, d), F32), jax.ShapeDtypeStruct((m, d), BF16)],
        scratch_shapes=[pltpu.VMEM((TOP_K * TC, d), F32), pltpu.SemaphoreType.DMA(())],
        compiler_params=_params(1, vmem + (8 << 20)),
        name="moe_combine_ln",
    )(pos, y, gates, x, g, b)


def _routing_tables(eid, TM, TC):
    L = eid.shape[1]
    n_assign = TOP_K * L
    tmax = n_assign // TM + N_EXPERTS
    flat = eid.reshape(-1)
    onehot = (flat[:, None] == jnp.arange(N_EXPERTS, dtype=jnp.int32)[None, :]).astype(jnp.int32)
    csum = jnp.cumsum(onehot, axis=0)
    counts = csum[-1]
    rank = jnp.sum((csum - onehot) * onehot, axis=1)
    tiles_per = (counts + TM - 1) // TM
    tile_end = jnp.cumsum(tiles_per)
    tile_start = tile_end - tiles_per
    pos = tile_start[flat] * TM + rank
    token = jnp.tile(jnp.arange(L, dtype=jnp.int32), TOP_K)
    row_token = jnp.zeros((tmax * TM,), jnp.int32).at[pos].set(token)
    tile_ids = jnp.arange(tmax, dtype=jnp.int32)
    tile_expert = jnp.minimum(jnp.sum((tile_ids[:, None] >= tile_end[None, :]).astype(jnp.int32), axis=1),
                              N_EXPERTS - 1).astype(jnp.int32)
    n_tiles = tile_end[-1:].astype(jnp.int32)
    pos_tiles = pos.reshape(TOP_K, L // TC, TC).transpose(1, 0, 2).reshape(L // TC, 1, TOP_K * TC)
    return row_token.reshape(tmax, 1, TM), pos_tiles.astype(jnp.int32), tile_expert, n_tiles


def _moe_block(x, xb, router_wt, router_b, w_gate, w_up, w_down, ln_g, ln_b, alpha):
    L = x.shape[0]
    TM = min(256, L)
    TC = min(128, L)
    eid, gates = moe_router(x, router_wt, router_b)
    row_token, pos, tile_expert, n_tiles = _routing_tables(eid, TM, TC)
    act = moe_up(x, w_gate.astype(BF16), w_up.astype(BF16), tile_expert, n_tiles, row_token, TM)
    y = moe_down(act, w_down.astype(BF16), tile_expert, n_tiles, TM)
    return moe_combine_ln(y, pos, gates.T, x, ln_g, ln_b, alpha, TC)


def _pad_cols(w, mult):
    n = w.shape[1]
    return jnp.pad(w, ((0, 0), (0, (-n) % mult)))


def _ab_mixer(x, xb, w_in, w_out, sink, pe, w1, b1, w2, b2, ln_g, ln_b, alpha):
    L, d = x.shape
    ha = d // (2 * HEAD_DIM)
    hb = d // (2 * HEAD_DIM)
    aq, akv, bq = ha * HEAD_DIM, A_KV_HEADS * HEAD_DIM, hb * HEAD_DIM
    bkv = N_BRANCH * 2 * HEAD_DIM
    s0, s1, s2, s3, s4 = aq, aq + akv, aq + 2 * akv, aq + 2 * akv + bq, aq + 2 * akv + bq + bkv
    w = jnp.concatenate([w_in[:, :s0], w_in[:, s2:s3], w_in[:, s0:s2], w_in[:, s3:s4], w_in[:, s4:]], axis=1)
    w = _pad_cols(w.astype(BF16), 512)
    proj = proj_matmul(xb, w, BF16)
    qa0, qb0 = 0, ha
    ka0 = ha + hb
    va0 = ka0 + A_KV_HEADS
    kvb0 = va0 + A_KV_HEADS
    gate_blk = kvb0 + 2 * N_BRANCH
    slopes_a = _alibi_slopes(ha)
    slopes_b = _alibi_slopes(hb)
    oa = banded_attention(proj, slopes_a, sink.astype(F32), q_base=qa0, k_blk=ka0, v_blk=va0,
                          n_kv=A_KV_HEADS, G=ha // A_KV_HEADS, window=A_WINDOW, has_sink=True)
    ng = L // CMP_STRIDE
    n_cmp = (L - CMP_LEN) // CMP_STRIDE + 1
    n_sel = L // SEL_BLOCK
    nsp = -(-n_sel // LANES) * LANES
    groups = proj[kvb0:kvb0 + 2].reshape(2, ng, CMP_STRIDE * HEAD_DIM)
    kvc = nsa_compress(groups, pe.reshape(2, 1, CMP_LEN * HEAD_DIM).astype(BF16), w1.astype(BF16),
                       b1.reshape(2, 1, CMP_HIDDEN).astype(F32), w2.astype(BF16),
                       b2.reshape(2, 1, HEAD_DIM).astype(F32))
    cstart = np.arange(ng) * CMP_STRIDE
    sstart = np.arange(nsp) * SEL_BLOCK
    overlap = ((cstart[:, None] < sstart[None, :] + SEL_BLOCK) & (cstart[:, None] + CMP_LEN > sstart[None, :])
               & (np.arange(ng)[:, None] < n_cmp) & (np.arange(nsp)[None, :] < n_sel))
    overlap = jnp.asarray(overlap.astype(np.float32), dtype=BF16)
    ob, sel = nsa_cmp_attention(proj, kvc, slopes_b, overlap, q_base=qb0, H=hb, gate_blk=gate_blk,
                                n_cmp=n_cmp, n_sel=n_sel)
    ob = nsa_sel_attention(proj, sel, ob, slopes_b, q_base=qb0, H=hb, k_blk=kvb0 + 2, v_blk=kvb0 + 3,
                           gate_blk=gate_blk)
    ob = banded_attention(proj, slopes_b, jnp.zeros((hb,), F32), q_base=qb0, k_blk=kvb0 + 4, v_blk=kvb0 + 5,
                          n_kv=1, G=hb, window=B_WINDOW, gate_blk=gate_blk, gate_col=2, oacc=ob)
    o = jnp.concatenate([oa, ob], axis=0)
    return outproj_ln(o, w_out.astype(BF16), x, ln_g, ln_b, alpha)


def _c_mixer(x, xb, w_in, w_out, ln_g, ln_b, alpha):
    d = x.shape[1]
    hc = d // HEAD_DIM
    qkv = proj_matmul(xb, w_in.astype(BF16), F32)
    o = dilated_attention(qkv, _alibi_slopes(hc), H=hc)
    return outproj_ln(o, w_out.astype(BF16), x, ln_g, ln_b, alpha)


def kernel(x, ab_w_in, ab_w_out, a_sink, nsa_cmp_pe, nsa_cmp_w1, nsa_cmp_b1, nsa_cmp_w2, nsa_cmp_b2, c_w_in, c_w_out, ln_mix_g, ln_mix_b, ln_ffn_g, ln_ffn_b, router_w, router_b, moe_w_gate, moe_w_up, moe_w_down):
    batch, L, d = x.shape
    depth = ln_mix_g.shape[0]
    alpha = float((2 * depth) ** 0.25)
    router_wt = router_w.T.astype(F32)
    router_bc = router_b.reshape(N_EXPERTS, 1).astype(F32)
    outs = []
    for bi in range(batch):
        xf = x[bi].astype(F32)
        xb = xf.astype(BF16)
        for layer in range(depth):
            i = layer // 2
            g_mix, b_mix = ln_mix_g[layer].reshape(1, d), ln_mix_b[layer].reshape(1, d)
            g_ffn, b_ffn = ln_ffn_g[layer].reshape(1, d), ln_ffn_b[layer].reshape(1, d)
            if layer % 2 == 0:
                xf, xb = _ab_mixer(xf, xb, ab_w_in[i], ab_w_out[i], a_sink[i], nsa_cmp_pe[i], nsa_cmp_w1[i],
                                   nsa_cmp_b1[i], nsa_cmp_w2[i], nsa_cmp_b2[i], g_mix, b_mix, alpha)
            else:
                xf, xb = _c_mixer(xf, xb, c_w_in[i], c_w_out[i], g_mix, b_mix, alpha)
            xf, xb = _moe_block(xf, xb, router_wt, router_bc, moe_w_gate[layer], moe_w_up[layer],
                                moe_w_down[layer], g_ffn, b_ffn, alpha)
        outs.append(xf)
    return jnp.stack(outs, axis=0).astype(x.dtype)
```

```python
import functools
import math

import numpy as np
import jax
import jax.numpy as jnp
from jax import lax
from jax.experimental import pallas as pl
from jax.experimental.pallas import tpu as pltpu

HEAD_DIM = 128
BLK = 128
A_KV_HEADS = 2
A_WINDOW = 128
N_BRANCH = 3
CMP_LEN = 32
CMP_STRIDE = 16
CMP_HIDDEN = 256
SEL_BLOCK = 64
SEL_TOPK = 16
B_WINDOW = 512
C_PATTERNS = ((128, 1), (512, 4), (2048, 16))
N_EXPERTS = 16
N_GROUPS = 4
EXPERTS_PER_GROUP = N_EXPERTS // N_GROUPS
TOP_K = 2
LN_EPS = 1e-5
NEG = -1e30
M_INIT = -1e20
LOG2E = math.log2(math.e)
Q_SCALE = HEAD_DIM ** -0.5 * LOG2E

LANES = 128
VMEM_BUDGET = 56 * 1024 * 1024

F32 = jnp.float32
BF16 = jnp.bfloat16
NT_DIMS = (((1,), (1,)), ((), ()))


def _params(n_grid, vmem_bytes):
    return pltpu.CompilerParams(
        dimension_semantics=("arbitrary",) * n_grid,
        vmem_limit_bytes=int(min(VMEM_BUDGET, max(vmem_bytes, 16 * 1024 * 1024))),
    )


def _alibi_slopes2(n):
    return jnp.asarray(2.0 ** (-8.0 * np.arange(1, n + 1) / n) * LOG2E, dtype=F32)


def _layer_norm_rows(z, g, b):
    mu = jnp.mean(z, axis=-1, keepdims=True)
    zc = z - mu
    var = jnp.mean(zc * zc, axis=-1, keepdims=True)
    return zc * lax.rsqrt(var + LN_EPS) * g + b


def _lane_tile(x, n):
    return x if n == LANES else jnp.concatenate([x] * (n // LANES), axis=1)


def _cast_kernel(w_ref, o_ref, *, chunks, n_used):
    dst = 0
    for src, size, scale in chunks:
        blk = w_ref[0, :, src:src + size]
        if scale != 1.0:
            blk = blk * scale
        o_ref[:, dst:dst + size] = blk.astype(o_ref.dtype)
        dst += size
    assert dst == n_used
    if n_used < o_ref.shape[1]:
        o_ref[:, n_used:] = jnp.zeros((o_ref.shape[0], o_ref.shape[1] - n_used), o_ref.dtype)


def cast_weight(w_stacked, layer, chunks=None, pad_to=1):
    _, k, n = w_stacked.shape
    if chunks is None:
        chunks = ((0, n, 1.0),)
    n_used = sum(c[1] for c in chunks)
    n_out = -(-n_used // pad_to) * pad_to
    tr = 128 if n > 8192 else 256
    tr = min(tr, k)
    kern = functools.partial(_cast_kernel, chunks=tuple(chunks), n_used=n_used)
    vmem = 2 * (tr * n * 4 + tr * n_out * 2) + tr * n * 4
    return pl.pallas_call(
        kern,
        grid=(k // tr,),
        in_specs=[pl.BlockSpec((1, tr, n), lambda r: (layer, r, 0))],
        out_specs=pl.BlockSpec((tr, n_out), lambda r: (r, 0)),
        out_shape=jax.ShapeDtypeStruct((k, n_out), BF16),
        compiler_params=_params(1, vmem + (8 << 20)),
        name="cast_weight",
    )(w_stacked)


def _proj_kernel(x_ref, w_ref, o_ref):
    acc = jnp.dot(x_ref[...], w_ref[...], preferred_element_type=F32)
    for c in range(o_ref.shape[0]):
        o_ref[c] = acc[:, c * LANES:(c + 1) * LANES].astype(o_ref.dtype)


def proj_matmul(x, w, out_dtype):
    m, k = x.shape
    n = w.shape[1]
    tm = min(1024, m)
    tn = min(512, n)
    osz = jnp.dtype(out_dtype).itemsize
    vmem = 2 * (tm * k * 2 + k * tn * 2 + tm * tn * osz) + 2 * tm * tn * 4
    return pl.pallas_call(
        _proj_kernel,
        grid=(m // tm, n // tn),
        in_specs=[pl.BlockSpec((tm, k), lambda i, j: (i, 0)),
                  pl.BlockSpec((k, tn), lambda i, j: (0, j))],
        out_specs=pl.BlockSpec((tn // LANES, tm, LANES), lambda i, j: (j, i, 0)),
        out_shape=jax.ShapeDtypeStruct((n // LANES, m, LANES), out_dtype),
        compiler_params=_params(2, vmem + (8 << 20)),
        name="proj_matmul",
    )(x, w)


def _banded_kernel(slopes_ref, sinks_ref, q_ref, kc_ref, kp_ref, vc_ref, vp_ref, *rest,
                   G, R, PR, window, has_sink, gate_col):
    if gate_col is not None:
        g_ref, oacc_ref, o_ref, kbuf, vbuf = rest
    else:
        o_ref, kbuf, vbuf = rest
    kv = pl.program_id(0)
    i = pl.program_id(1)
    kn = PR + BLK
    kbuf[0:PR, :] = kp_ref[0]
    kbuf[PR:PR + R, :] = kc_ref[0]
    vbuf[0:PR, :] = vp_ref[0]
    vbuf[PR:PR + R, :] = vc_ref[0]
    qi = lax.broadcasted_iota(jnp.int32, (BLK, kn), 0)
    kc = lax.broadcasted_iota(jnp.int32, (BLK, kn), 1)
    dist = PR + qi - kc
    static_valid = (dist >= 0) & (dist < window)
    negd = -dist.astype(F32)

    def unit(j, carry):
        row0 = pl.multiple_of(j * BLK, BLK)
        first_key = PR - (i * R + j * BLK)
        valid = static_valid & (kc >= first_key)
        k = kbuf[pl.ds(row0, kn), :]
        v = vbuf[pl.ds(row0, kn), :]
        q = q_ref[:, pl.ds(row0, BLK), :].reshape(G * BLK, HEAD_DIM)
        s_all = lax.dot_general(q, k, NT_DIMS, preferred_element_type=F32)
        ps, ls = [], []
        for g in range(G):
            h = kv * G + g
            s = s_all[g * BLK:(g + 1) * BLK] + slopes_ref[h] * negd
            s = jnp.where(valid, s, NEG)
            m = jnp.max(s, axis=-1, keepdims=True)
            p = jnp.exp2(s - m)
            l = jnp.sum(p, axis=-1, keepdims=True)
            if has_sink:
                l = l + jnp.exp2(sinks_ref[h] - m)
            ps.append(p.astype(BF16))
            ls.append(l)
        acc = jnp.dot(jnp.concatenate(ps, axis=0), v, preferred_element_type=F32)
        for g in range(G):
            o = acc[g * BLK:(g + 1) * BLK] / ls[g]
            if gate_col is not None:
                gl = g_ref[0, pl.ds(row0, BLK), :].astype(F32)
                c = g * N_BRANCH + gate_col
                o = oacc_ref[g, pl.ds(row0, BLK), :] + jax.nn.sigmoid(gl[:, c:c + 1]) * o
            o_ref[g, pl.ds(row0, BLK), :] = o.astype(o_ref.dtype)
        return carry

    lax.fori_loop(0, R // BLK, unit, 0)


def banded_attention(proj, slopes2, sinks2, *, q_base, k_blk, v_blk, n_kv, G, window,
                     has_sink=False, gate_blk=None, gate_col=None, oacc=None):
    L = proj.shape[1]
    n_prev = -(-(window - 1) // BLK)
    PR = n_prev * BLK
    R = max(PR, min(L, 8192 // G))
    assert L % R == 0 and R % PR == 0 and q_base % G == 0
    qb0 = q_base // G
    rp = R // PR
    in_specs = [
        pl.BlockSpec((G, R, LANES), lambda kv, i, *_: (qb0 + kv, i, 0)),
        pl.BlockSpec((1, R, LANES), lambda kv, i, *_: (k_blk + kv, i, 0)),
        pl.BlockSpec((1, PR, LANES), lambda kv, i, *_: (k_blk + kv, jnp.maximum(i * rp - 1, 0), 0)),
        pl.BlockSpec((1, R, LANES), lambda kv, i, *_: (v_blk + kv, i, 0)),
        pl.BlockSpec((1, PR, LANES), lambda kv, i, *_: (v_blk + kv, jnp.maximum(i * rp - 1, 0), 0)),
    ]
    args = [proj, proj, proj, proj, proj]
    vmem = 2 * (G * R * LANES * 2 * 2 + 4 * (R + PR) * LANES * 2) + 2 * (R + PR) * LANES * 2
    if gate_col is not None:
        assert n_kv == 1
        in_specs += [pl.BlockSpec((1, R, LANES), lambda kv, i, *_: (gate_blk, i, 0)),
                     pl.BlockSpec((G, R, LANES), lambda kv, i, *_: (kv, i, 0))]
        args += [proj, oacc]
        vmem += 2 * (R * LANES * 2 + G * R * LANES * 4)
    kern = functools.partial(_banded_kernel, G=G, R=R, PR=PR, window=window,
                             has_sink=has_sink, gate_col=gate_col)
    return pl.pallas_call(
        kern,
        grid_spec=pltpu.PrefetchScalarGridSpec(
            num_scalar_prefetch=2,
            grid=(n_kv, L // R),
            in_specs=in_specs,
            out_specs=pl.BlockSpec((G, R, LANES), lambda kv, i, *_: (kv, i, 0)),
            scratch_shapes=[pltpu.VMEM((PR + R, LANES), BF16), pltpu.VMEM((PR + R, LANES), BF16)],
        ),
        out_shape=jax.ShapeDtypeStruct((n_kv * G, L, LANES), BF16),
        compiler_params=_params(2, vmem + (16 << 20)),
        name="banded_attention",
    )(slopes2, sinks2, *args)


def _compress_kernel(g_ref, pe_ref, w1_ref, b1_ref, w2_ref, b2_ref, o_ref):
    ng = g_ref.shape[1]
    half = w1_ref.shape[1] // 2
    g = g_ref[0]
    y0 = jnp.dot(g, w1_ref[0, 0:half, :], preferred_element_type=F32)
    y1 = jnp.dot(g, w1_ref[0, half:, :], preferred_element_type=F32)
    y1 = pltpu.roll(y1, ng - 1, 0)
    pe = jnp.broadcast_to(pe_ref[0], (8, pe_ref.shape[2]))
    ype = jnp.dot(pe, w1_ref[0], preferred_element_type=F32)[0:1]
    h = jax.nn.gelu(y0 + y1 + ype + b1_ref[0])
    out = jnp.dot(h.astype(BF16), w2_ref[0], preferred_element_type=F32) + b2_ref[0]
    o_ref[0] = out.astype(o_ref.dtype)


def nsa_compress(groups, pe, w1, b1, w2, b2):
    _, ng, gw = groups.shape
    hid = w1.shape[2]
    vmem = 2 * (ng * gw * 2 + 2 * gw * hid * 2 + 2 * gw * 2) + 6 * ng * hid * 4
    return pl.pallas_call(
        _compress_kernel,
        grid=(2,),
        in_specs=[pl.BlockSpec((1, ng, gw), lambda c: (c, 0, 0)),
                  pl.BlockSpec((1, 1, 2 * gw), lambda c: (c, 0, 0)),
                  pl.BlockSpec((1, 2 * gw, hid), lambda c: (c, 0, 0)),
                  pl.BlockSpec((1, 1, hid), lambda c: (c, 0, 0)),
                  pl.BlockSpec((1, hid, HEAD_DIM), lambda c: (c, 0, 0)),
                  pl.BlockSpec((1, 1, HEAD_DIM), lambda c: (c, 0, 0))],
        out_specs=pl.BlockSpec((1, ng, HEAD_DIM), lambda c: (c, 0, 0)),
        out_shape=jax.ShapeDtypeStruct((2, ng, HEAD_DIM), BF16),
        compiler_params=_params(1, vmem + (8 << 20)),
        name="nsa_compress",
    )(groups, pe, w1, b1, w2, b2)


def _cmp_kernel(slopes_ref, q_ref, kv_ref, g_ref, ov_ref, o_ref, sel_ref, *, H, TQ, n_cmp, n_sel):
    qb = pl.program_id(0)
    ncp = kv_ref.shape[1]
    nsp = sel_ref.shape[1]
    kc = kv_ref[0]
    vc = kv_ref[1]
    t = qb * TQ + lax.broadcasted_iota(jnp.int32, (TQ, 1), 0)
    n = lax.broadcasted_iota(jnp.int32, (1, ncp), 1)
    cdist = t - (n * CMP_STRIDE + (CMP_LEN - 1))
    valid = (cdist >= 0) & (n < n_cmp)
    negd = -jnp.maximum(cdist, 0).astype(F32)
    gl = g_ref[0].astype(F32)
    psum = jnp.zeros((TQ, ncp), F32)
    for h in range(H):
        s = lax.dot_general(q_ref[h], kc, NT_DIMS, preferred_element_type=F32)
        s = jnp.where(valid, s + slopes_ref[h] * negd, NEG)
        m = jnp.max(s, axis=-1, keepdims=True)
        e = jnp.exp2(s - m)
        l = jnp.sum(e, axis=-1, keepdims=True)
        p = jnp.where(valid, e / l, 0.0)
        psum = psum + p
        o = jnp.dot(p.astype(BF16), vc, preferred_element_type=F32)
        c = h * N_BRANCH
        o_ref[h] = jax.nn.sigmoid(gl[:, c:c + 1]) * o
    p_hi = psum.astype(BF16)
    p_lo = (psum - p_hi.astype(F32)).astype(BF16)
    imp = (jnp.dot(p_hi, ov_ref[...], preferred_element_type=F32)
           + jnp.dot(p_lo, ov_ref[...], preferred_element_type=F32))
    j = lax.broadcasted_iota(jnp.int32, (1, nsp), 1)
    jf = j.astype(F32)
    cur = t // SEL_BLOCK
    forced = (j == 0) | (j == cur) | (j == cur - 1)
    svalid = j * SEL_BLOCK <= t
    work = jnp.where(forced, 1e30, jnp.where(svalid, imp, -1.0))
    work = jnp.where(j < n_sel, work, -2.0)
    sel = jnp.zeros((TQ, nsp), F32)
    for _ in range(min(SEL_TOPK, n_sel)):
        mx = jnp.max(work, axis=-1, keepdims=True)
        idx = jnp.min(jnp.where(work == mx, jf, 1e9), axis=-1, keepdims=True)
        hit = jf == idx
        sel = jnp.where(hit, 1.0, sel)
        work = jnp.where(hit, -2.0, work)
    sel_ref[...] = sel.astype(sel_ref.dtype)


def nsa_cmp_attention(proj, kvc, slopes2, overlap, *, q_base, H, gate_blk, n_cmp, n_sel):
    L = proj.shape[1]
    ncp = kvc.shape[1]
    nsp = overlap.shape[1]
    TQ = min(256, L)
    kern = functools.partial(_cmp_kernel, H=H, TQ=TQ, n_cmp=n_cmp, n_sel=n_sel)
    vmem = 2 * (H * TQ * LANES * 2 + 2 * ncp * LANES * 2 + TQ * LANES * 2 + ncp * nsp * 2
                + H * TQ * LANES * 4 + TQ * nsp * 2) + 8 * TQ * ncp * 4
    return pl.pallas_call(
        kern,
        grid_spec=pltpu.PrefetchScalarGridSpec(
            num_scalar_prefetch=1,
            grid=(L // TQ,),
            in_specs=[pl.BlockSpec((H, TQ, LANES), lambda qb, *_: (q_base // H, qb, 0)),
                      pl.BlockSpec((2, ncp, LANES), lambda qb, *_: (0, 0, 0)),
                      pl.BlockSpec((1, TQ, LANES), lambda qb, *_: (gate_blk, qb, 0)),
                      pl.BlockSpec((ncp, nsp), lambda qb, *_: (0, 0))],
            out_specs=[pl.BlockSpec((H, TQ, LANES), lambda qb, *_: (0, qb, 0)),
                       pl.BlockSpec((TQ, nsp), lambda qb, *_: (qb, 0))],
        ),
        out_shape=[jax.ShapeDtypeStruct((H, L, LANES), F32),
                   jax.ShapeDtypeStruct((L, nsp), BF16)],
        compiler_params=_params(1, vmem + (8 << 20)),
        name="nsa_cmp_attention",
    )(slopes2, proj, kvc, proj, overlap)


def _sel_kernel(q_ref, qx_ref, k_ref, v_ref, sel_ref, g_ref, oacc_ref, o_ref, qs, m_s, acc_s, *, H, TQ, TK):
    qb = pl.program_id(0)
    kb = pl.program_id(1)
    nkb = pl.num_programs(1)
    nsp = sel_ref.shape[1]

    @pl.when(kb == 0)
    def _():
        for h in range(H):
            qs[h * TQ:(h + 1) * TQ, 0:LANES] = q_ref[h]
            qs[h * TQ:(h + 1) * TQ, LANES:2 * LANES] = jnp.broadcast_to(qx_ref[h], (TQ, LANES))
        m_s[...] = jnp.full(m_s.shape, M_INIT, F32)
        acc_s[...] = jnp.zeros(acc_s.shape, F32)

    @pl.when(kb * TK <= qb * TQ + (TQ - 1))
    def _():
        kx = k_ref[...]
        vx = v_ref[...]
        t = qb * TQ + lax.broadcasted_iota(jnp.int32, (TQ, 1), 0)
        kk = kb * TK + lax.broadcasted_iota(jnp.int32, (1, TK), 1)
        jrow = lax.broadcasted_iota(jnp.int32, (nsp, 1), 0)
        expand = jnp.where(jrow == kk // SEL_BLOCK, 1.0, 0.0).astype(BF16)
        chosen = jnp.dot(sel_ref[...], expand, preferred_element_type=F32)
        mask = (chosen > 0.5) & (kk <= t)
        s_all = lax.dot_general(qs[...], kx, NT_DIMS, preferred_element_type=F32)
        ps = []
        for h in range(H):
            rows = slice(h * TQ, (h + 1) * TQ)
            s = jnp.where(mask, s_all[rows], NEG)
            m_old = m_s[rows, :]
            m_new = jnp.maximum(m_old, jnp.max(s, axis=-1, keepdims=True))
            m_s[rows, :] = m_new
            alpha = jnp.exp2(m_old - m_new)
            acc_s[rows, :] = acc_s[rows, :] * _lane_tile(alpha, 2 * LANES)
            ps.append(jnp.exp2(s - _lane_tile(m_new, TK)).astype(BF16))
        acc_s[...] += jnp.dot(jnp.concatenate(ps, axis=0), vx, preferred_element_type=F32)

    @pl.when(kb == nkb - 1)
    def _():
        gl = g_ref[0].astype(F32)
        for h in range(H):
            rows = slice(h * TQ, (h + 1) * TQ)
            c = h * N_BRANCH + 1
            o = acc_s[rows, 0:LANES] / acc_s[rows, LANES:2 * LANES]
            o_ref[h] = oacc_ref[h] + jax.nn.sigmoid(gl[:, c:c + 1]) * o


def nsa_sel_attention(proj, kx, vx, qx, sel, oacc, *, q_base, H, gate_blk):
    L = proj.shape[1]
    nsp = sel.shape[1]
    TQ = min(128, L)
    TK = min(1024, L)

    def kv_idx(qb, kb):
        return (jnp.minimum(kb, (qb * TQ + TQ - 1) // TK), 0)

    kern = functools.partial(_sel_kernel, H=H, TQ=TQ, TK=TK)
    vmem = (2 * (H * TQ * LANES * 2 + 2 * TK * 2 * LANES * 2 + TQ * nsp * 2 + TQ * LANES * 2
                 + 2 * H * TQ * LANES * 4) + H * TQ * LANES * (4 + 4 + 8)
            + H * TQ * TK * (4 + 4 + 2))
    return pl.pallas_call(
        kern,
        grid=(L // TQ, L // TK),
        in_specs=[pl.BlockSpec((H, TQ, LANES), lambda qb, kb: (q_base // H, qb, 0)),
                  pl.BlockSpec((H, 1, LANES), lambda qb, kb: (0, 0, 0)),
                  pl.BlockSpec((TK, 2 * LANES), kv_idx),
                  pl.BlockSpec((TK, 2 * LANES), kv_idx),
                  pl.BlockSpec((TQ, nsp), lambda qb, kb: (qb, 0)),
                  pl.BlockSpec((1, TQ, LANES), lambda qb, kb: (gate_blk, qb, 0)),
                  pl.BlockSpec((H, TQ, LANES), lambda qb, kb: (0, qb, 0))],
        out_specs=pl.BlockSpec((H, TQ, LANES), lambda qb, kb: (0, qb, 0)),
        scratch_shapes=[pltpu.VMEM((H * TQ, 2 * LANES), BF16), pltpu.VMEM((H * TQ, LANES), F32),
                        pltpu.VMEM((H * TQ, 2 * LANES), F32)],
        out_shape=jax.ShapeDtypeStruct((H, L, LANES), F32),
        compiler_params=_params(2, vmem + (4 << 20)),
        name="nsa_sel_attention",
    )(proj, qx, kx, vx, sel, proj, oacc)


def _sel_position_columns(L, slopes2):
    pos = np.arange(L)
    c1 = (pos // LANES) * LANES
    c2 = pos % LANES
    kcols = np.zeros((L, LANES), np.float32)
    kcols[:, 0:3] = c1[:, None]
    kcols[:, 3:6] = c2[:, None]
    a1 = slopes2.astype(BF16)
    r1 = slopes2 - a1.astype(F32)
    a2 = r1.astype(BF16)
    a3 = (r1 - a2.astype(F32)).astype(BF16)
    parts = jnp.stack([a1, a2, a3, a1, a2, a3], axis=1)
    qx = jnp.zeros((slopes2.shape[0], LANES), BF16).at[:, 0:6].set(parts)
    return jnp.asarray(kcols, dtype=BF16), qx.reshape(-1, 1, LANES)


def _dilated_kernel(slopes_ref, q_ref, k_ref, v_ref, o_ref, q4, k4, v4, ktail, vtail, o_s, l_s, *, CH):
    h = pl.program_id(0)
    i = pl.program_id(1)
    par = i % 2
    slope = slopes_ref[h]
    sub = CH // 4

    @pl.when(i == 0)
    def _():
        k4[1] = jnp.zeros(k4.shape[1:], F32)
        v4[1] = jnp.zeros(v4.shape[1:], F32)
        ktail[...] = jnp.zeros(ktail.shape, F32)
        vtail[...] = jnp.zeros(vtail.shape, F32)

    for r in range(4):
        q4[r] = q_ref[0, pl.ds(r, sub, stride=4), :]
        k4[par, r] = k_ref[0, pl.ds(r, sub, stride=4), :]
        v4[par, r] = v_ref[0, pl.ds(r, sub, stride=4), :]

    kn = 2 * BLK
    qi = lax.broadcasted_iota(jnp.int32, (BLK, kn), 0)
    kc = lax.broadcasted_iota(jnp.int32, (BLK, kn), 1)
    dist = BLK + qi - kc
    static_valid = (dist >= 0) & (dist <= BLK)
    first_valid = static_valid & (kc >= jnp.where(i == 0, BLK, 0))
    negd = -dist.astype(F32)
    ones = jnp.ones((kn, LANES), BF16)

    def unit(p, bias, q, kprev, kcur, vprev, vcur, rows):
        kk = jnp.concatenate([kprev.astype(BF16), kcur.astype(BF16)], axis=0)
        vv = jnp.concatenate([vprev.astype(BF16), vcur.astype(BF16)], axis=0)
        s = lax.dot_general(q.astype(BF16), kk, NT_DIMS, preferred_element_type=F32) + bias
        m = jnp.max(s, axis=-1, keepdims=True)
        e = jnp.exp2(s - m).astype(BF16)
        acc = jnp.dot(e, jnp.concatenate([vv, ones], axis=1), preferred_element_type=F32)
        l = acc[:, LANES:]
        o_s[p, rows, :] = acc[:, :LANES] / l
        l_s[p, rows, :] = m + jnp.log2(l)

    for p, (window, dil) in enumerate(C_PATTERNS):
        assert window // dil == BLK
        ab = (slope * dil) * negd
        bias = jnp.where(static_valid, ab, NEG)
        bias0 = jnp.where(first_valid, ab, NEG)
        if dil == 1:
            for j in range(CH // BLK):
                cur = slice(j * BLK, (j + 1) * BLK)
                prev = slice((j - 1) * BLK, j * BLK)
                unit(p, bias if j else bias0, q_ref[0, cur, :],
                     k_ref[0, prev, :] if j else ktail[...], k_ref[0, cur, :],
                     v_ref[0, prev, :] if j else vtail[...], v_ref[0, cur, :], pl.ds(j * BLK, BLK))
        elif dil == 4:
            for r in range(4):
                for j in range(sub // BLK):
                    cur = slice(j * BLK, (j + 1) * BLK)
                    prev = slice((j - 1) * BLK, j * BLK)
                    last = slice(sub - BLK, sub)
                    unit(p, bias if j else bias0, q4[r, cur, :],
                         k4[par, r, prev, :] if j else k4[1 - par, r, last, :], k4[par, r, cur, :],
                         v4[par, r, prev, :] if j else v4[1 - par, r, last, :], v4[par, r, cur, :],
                         pl.ds(r + 4 * j * BLK, BLK, stride=4))
        else:
            assert dil == 16 and sub == 4 * BLK
            for r in range(16):
                r4, a = r % 4, r // 4
                rows4 = pl.ds(a, BLK, stride=4)
                unit(p, bias0, q4[r4, rows4, :],
                     k4[1 - par, r4, rows4, :], k4[par, r4, rows4, :],
                     v4[1 - par, r4, rows4, :], v4[par, r4, rows4, :], pl.ds(r, BLK, stride=16))

    ktail[...] = k_ref[0, CH - BLK:CH, :]
    vtail[...] = v_ref[0, CH - BLK:CH, :]
    lses = [l_s[p] for p in range(len(C_PATTERNS))]
    mx = functools.reduce(jnp.maximum, lses)
    ws = [jnp.exp2(x - mx) for x in lses]
    num = functools.reduce(lambda a, b: a + b, [w * o_s[p] for p, w in enumerate(ws)])
    den = functools.reduce(lambda a, b: a + b, ws)
    o_ref[0] = (num / den).astype(o_ref.dtype)


def dilated_attention(qkv, slopes2, *, H):
    L = qkv.shape[1]
    CH = max(d for _, d in C_PATTERNS) * BLK
    assert L % CH == 0
    kern = functools.partial(_dilated_kernel, CH=CH)
    npat = len(C_PATTERNS)
    vmem = 2 * (3 * CH * LANES * 4 + CH * LANES * 2) + (5 + 2 * npat) * CH * LANES * 4
    return pl.pallas_call(
        kern,
        grid_spec=pltpu.PrefetchScalarGridSpec(
            num_scalar_prefetch=1,
            grid=(H, L // CH),
            in_specs=[pl.BlockSpec((1, CH, LANES), lambda h, i, *_: (h, i, 0)),
                      pl.BlockSpec((1, CH, LANES), lambda h, i, *_: (H + h, i, 0)),
                      pl.BlockSpec((1, CH, LANES), lambda h, i, *_: (2 * H + h, i, 0))],
            out_specs=pl.BlockSpec((1, CH, LANES), lambda h, i, *_: (h, i, 0)),
            scratch_shapes=[pltpu.VMEM((4, CH // 4, LANES), F32),
                            pltpu.VMEM((2, 4, CH // 4, LANES), F32),
                            pltpu.VMEM((2, 4, CH // 4, LANES), F32),
                            pltpu.VMEM((BLK, LANES), F32), pltpu.VMEM((BLK, LANES), F32),
                            pltpu.VMEM((npat, CH, LANES), F32), pltpu.VMEM((npat, CH, LANES), F32)],
        ),
        out_shape=jax.ShapeDtypeStruct((H, L, LANES), BF16),
        compiler_params=_params(2, vmem + (16 << 20)),
        name="dilated_attention",
    )(slopes2, qkv, qkv, qkv)


def _outproj_kernel(o_ref, w_ref, y_ref):
    lhs = jnp.concatenate([o_ref[c] for c in range(o_ref.shape[0])], axis=1)
    y_ref[...] = jnp.dot(lhs, w_ref[...], preferred_element_type=F32)


def outproj(o_heads, w):
    nh, m, _ = o_heads.shape
    k, n = w.shape
    assert k == nh * LANES
    tm = min(1024, m)
    tn = min(512, n)
    vmem = 2 * (tm * k * 2 + k * tn * 2 + tm * tn * 4) + tm * k * 2 + tm * tn * 4
    return pl.pallas_call(
        _outproj_kernel,
        grid=(m // tm, n // tn),
        in_specs=[pl.BlockSpec((nh, tm, LANES), lambda i, j: (0, i, 0)),
                  pl.BlockSpec((k, tn), lambda i, j: (0, j))],
        out_specs=pl.BlockSpec((tm, tn), lambda i, j: (i, j)),
        out_shape=jax.ShapeDtypeStruct((m, n), F32),
        compiler_params=_params(2, vmem + (8 << 20)),
        name="outproj",
    )(o_heads, w)


def _route(x, wt, b):
    logits = lax.dot_general(wt, x, NT_DIMS, preferred_element_type=F32,
                             precision=lax.Precision.HIGHEST) + b
    mx = jnp.max(logits, axis=0, keepdims=True)
    e = jnp.exp(logits - mx)
    probs = e / jnp.sum(e, axis=0, keepdims=True)
    rows = [probs[i:i + 1, :] for i in range(N_EXPERTS)]
    epg = EXPERTS_PER_GROUP
    best = None
    gsel = None
    for gi in range(N_GROUPS):
        grp = rows[gi * epg:(gi + 1) * epg]
        score = None
        for a in range(epg):
            for c in range(a + 1, epg):
                pair = grp[a] + grp[c]
                score = pair if score is None else jnp.maximum(score, pair)
        if best is None:
            best, gsel = score, jnp.zeros(score.shape, jnp.int32)
        else:
            better = score > best
            gsel = jnp.where(better, gi, gsel)
            best = jnp.maximum(best, score)
    ing = []
    for kk in range(epg):
        val = rows[kk]
        for gi in range(1, N_GROUPS):
            val = jnp.where(gsel == gi, rows[gi * epg + kk], val)
        ing.append(val)

    def argmax_first(vals):
        bv, bi = vals[0], jnp.zeros(vals[0].shape, jnp.int32)
        for kk in range(1, len(vals)):
            better = vals[kk] > bv
            bi = jnp.where(better, kk, bi)
            bv = jnp.maximum(bv, vals[kk])
        return bv, bi

    v1, i1 = argmax_first(ing)
    rest = [jnp.where(i1 == kk, -1.0, ing[kk]) for kk in range(epg)]
    v2, i2 = argmax_first(rest)
    tot = v1 + v2
    return (gsel * epg + i1, gsel * epg + i2), (v1 / tot, v2 / tot)


def _ln_router_kernel(y_ref, x_ref, g_ref, b_ref, wt_ref, rb_ref, xo_ref, xb_ref, eid_ref, gate_ref, *, alpha):
    xn = _layer_norm_rows(alpha * x_ref[...] + y_ref[...], g_ref[...], b_ref[...])
    xo_ref[...] = xn
    xb_ref[...] = xn.astype(BF16)
    eids, gates = _route(xn, wt_ref[...], rb_ref[...])
    for kk in range(TOP_K):
        eid_ref[kk:kk + 1, :] = eids[kk]
        gate_ref[kk:kk + 1, :] = gates[kk]


def ln_router(y, x, g, b, wt, rb, alpha):
    m, d = x.shape
    tr = min(256, m)
    kern = functools.partial(_ln_router_kernel, alpha=alpha)
    vmem = 2 * (3 * tr * d * 4 + tr * d * 2 + N_EXPERTS * d * 4) + 8 * tr * d * 2 + 3 * tr * d * 4
    return pl.pallas_call(
        kern,
        grid=(m // tr,),
        in_specs=[pl.BlockSpec((tr, d), lambda i: (i, 0)),
                  pl.BlockSpec((tr, d), lambda i: (i, 0)),
                  pl.BlockSpec((1, d), lambda i: (0, 0)),
                  pl.BlockSpec((1, d), lambda i: (0, 0)),
                  pl.BlockSpec((N_EXPERTS, d), lambda i: (0, 0)),
                  pl.BlockSpec((N_EXPERTS, 1), lambda i: (0, 0))],
        out_specs=[pl.BlockSpec((tr, d), lambda i: (i, 0)),
                   pl.BlockSpec((tr, d), lambda i: (i, 0)),
                   pl.BlockSpec((TOP_K, tr), lambda i: (0, i)),
                   pl.BlockSpec((TOP_K, tr), lambda i: (0, i))],
        out_shape=[jax.ShapeDtypeStruct((m, d), F32), jax.ShapeDtypeStruct((m, d), BF16),
                   jax.ShapeDtypeStruct((TOP_K, m), jnp.int32), jax.ShapeDtypeStruct((TOP_K, m), F32)],
        compiler_params=_params(1, vmem + (4 << 20)),
        name="ln_router",
    )(y, x, g, b, wt, rb)


def _row_copy(src_hbm, row, dst, slot, r, sem):
    return pltpu.make_async_copy(src_hbm.at[pl.ds(row, 1)], dst.at[slot, pl.ds(r, 1)], sem.at[slot])


def _gather_start(src_hbm, idx_ref, dst, slot, n_rows, sem):
    def body(r, c):
        _row_copy(src_hbm, idx_ref[0, 0, r], dst, slot, r, sem).start()
        return c
    lax.fori_loop(0, n_rows, body, 0)


def _gather_wait(src_hbm, dst, slot, n_rows, sem):
    def body(r, c):
        _row_copy(src_hbm, 0, dst, slot, r, sem).wait()
        return c
    lax.fori_loop(0, n_rows, body, 0)


def _moe_up_kernel(te_ref, nt_ref, idx_ref, idx_next_ref, x_hbm, wg_ref, wu_ref, act_ref, xbuf, sem, *, TM):
    t = pl.program_id(0)
    nt = nt_ref[0]
    slot = t % 2

    @pl.when((t == 0) & (nt > 0))
    def _():
        _gather_start(x_hbm, idx_ref, xbuf, 0, TM, sem)

    @pl.when(t + 1 < nt)
    def _():
        _gather_start(x_hbm, idx_next_ref, xbuf, 1 - slot, TM, sem)

    @pl.when(t < nt)
    def _():
        _gather_wait(x_hbm, xbuf, slot, TM, sem)
        xb = xbuf[slot].astype(BF16)
        hg = jnp.dot(xb, wg_ref[0], preferred_element_type=F32)
        hu = jnp.dot(xb, wu_ref[0], preferred_element_type=F32)
        act_ref[...] = (jax.nn.silu(hg) * hu).astype(act_ref.dtype)

    @pl.when(t >= nt)
    def _():
        act_ref[...] = jnp.zeros(act_ref.shape, act_ref.dtype)


def moe_up(x, w_gate, w_up, tile_expert, n_tiles, row_token, TM):
    d = x.shape[1]
    f = w_gate.shape[2]
    tmax = tile_expert.shape[0]
    kern = functools.partial(_moe_up_kernel, TM=TM)
    vmem = 2 * (2 * d * f * 2 + TM * f * 2) + 2 * TM * d * 4 + TM * d * 2 + 4 * TM * f * 4
    return pl.pallas_call(
        kern,
        grid_spec=pltpu.PrefetchScalarGridSpec(
            num_scalar_prefetch=2,
            grid=(tmax,),
            in_specs=[pl.BlockSpec((1, 1, TM), lambda t, te, nt: (t, 0, 0), memory_space=pltpu.SMEM),
                      pl.BlockSpec((1, 1, TM), lambda t, te, nt: (jnp.minimum(t + 1, tmax - 1), 0, 0),
                                   memory_space=pltpu.SMEM),
                      pl.BlockSpec(memory_space=pl.ANY),
                      pl.BlockSpec((1, d, f), lambda t, te, nt: (te[t], 0, 0)),
                      pl.BlockSpec((1, d, f), lambda t, te, nt: (te[t], 0, 0))],
            out_specs=pl.BlockSpec((TM, f), lambda t, te, nt: (t, 0)),
            scratch_shapes=[pltpu.VMEM((2, TM, d), F32), pltpu.SemaphoreType.DMA((2,))],
        ),
        out_shape=jax.ShapeDtypeStruct((tmax * TM, f), BF16),
        compiler_params=_params(1, vmem + (6 << 20)),
        name="moe_up",
    )(tile_expert, n_tiles, row_token, row_token, x, w_gate, w_up)


def _moe_down_kernel(te_ref, nt_ref, act_ref, wd_ref, y_ref):
    t = pl.program_id(0)

    @pl.when(t < nt_ref[0])
    def _():
        y_ref[...] = jnp.dot(act_ref[...], wd_ref[0], preferred_element_type=F32)

    @pl.when(t >= nt_ref[0])
    def _():
        y_ref[...] = jnp.zeros(y_ref.shape, y_ref.dtype)


def moe_down(act, w_down, tile_expert, n_tiles, TM):
    f = act.shape[1]
    d = w_down.shape[2]
    tmax = tile_expert.shape[0]
    vmem = 2 * (TM * f * 2 + f * d * 2 + TM * d * 4) + TM * d * 4
    return pl.pallas_call(
        _moe_down_kernel,
        grid_spec=pltpu.PrefetchScalarGridSpec(
            num_scalar_prefetch=2,
            grid=(tmax,),
            in_specs=[pl.BlockSpec((TM, f), lambda t, te, nt: (t, 0)),
                      pl.BlockSpec((1, f, d), lambda t, te, nt: (te[t], 0, 0))],
            out_specs=pl.BlockSpec((TM, d), lambda t, te, nt: (t, 0)),
        ),
        out_shape=jax.ShapeDtypeStruct((tmax * TM, d), F32),
        compiler_params=_params(1, vmem + (8 << 20)),
        name="moe_down",
    )(tile_expert, n_tiles, act, w_down)


def _combine_ln_kernel(pos_ref, pos_next_ref, y_hbm, gate_ref, x_ref, g_ref, b_ref, xo_ref, xb_ref, ybuf, sem,
                       *, TC, alpha):
    i = pl.program_id(0)
    n = pl.num_programs(0)
    slot = i % 2
    rows = TOP_K * TC

    @pl.when(i == 0)
    def _():
        _gather_start(y_hbm, pos_ref, ybuf, 0, rows, sem)

    @pl.when(i + 1 < n)
    def _():
        _gather_start(y_hbm, pos_next_ref, ybuf, 1 - slot, rows, sem)

    _gather_wait(y_hbm, ybuf, slot, rows, sem)
    gate = gate_ref[...]
    ffn = gate[:, 0:1] * ybuf[slot, 0:TC, :]
    for kk in range(1, TOP_K):
        ffn = ffn + gate[:, kk:kk + 1] * ybuf[slot, kk * TC:(kk + 1) * TC, :]
    y = _layer_norm_rows(alpha * x_ref[...] + ffn, g_ref[...], b_ref[...])
    xo_ref[...] = y
    xb_ref[...] = y.astype(BF16)


def moe_combine_ln(y, pos, gates, x, g, b, alpha, TC):
    m, d = x.shape
    nt = m // TC
    kern = functools.partial(_combine_ln_kernel, TC=TC, alpha=alpha)
    vmem = 2 * (TC * LANES * 4 + 2 * TC * d * 4 + TC * d * 2) + 2 * TOP_K * TC * d * 4 + 3 * TC * d * 4
    return pl.pallas_call(
        kern,
        grid=(nt,),
        in_specs=[pl.BlockSpec((1, 1, TOP_K * TC), lambda i: (i, 0, 0), memory_space=pltpu.SMEM),
                  pl.BlockSpec((1, 1, TOP_K * TC), lambda i: (jnp.minimum(i + 1, nt - 1), 0, 0),
                               memory_space=pltpu.SMEM),
                  pl.BlockSpec(memory_space=pl.ANY),
                  pl.BlockSpec((TC, TOP_K), lambda i: (i, 0)),
                  pl.BlockSpec((TC, d), lambda i: (i, 0)),
                  pl.BlockSpec((1, d), lambda i: (0, 0)),
                  pl.BlockSpec((1, d), lambda i: (0, 0))],
        out_specs=[pl.BlockSpec((TC, d), lambda i: (i, 0)),
                   pl.BlockSpec((TC, d), lambda i: (i, 0))],
        out_shape=[jax.ShapeDtypeStruct((m, d), F32), jax.ShapeDtypeStruct((m, d), BF16)],
        scratch_shapes=[pltpu.VMEM((2, TOP_K * TC, d), F32), pltpu.SemaphoreType.DMA((2,))],
        compiler_params=_params(1, vmem + (8 << 20)),
        name="moe_combine_ln",
    )(pos, pos, y, gates, x, g, b)


def _routing_tables(eid, TM, TC):
    L = eid.shape[1]
    n_assign = TOP_K * L
    tmax = n_assign // TM + N_EXPERTS
    flat = eid.reshape(-1)
    onehot = (flat[:, None] == jnp.arange(N_EXPERTS, dtype=jnp.int32)[None, :]).astype(jnp.int32)
    csum = jnp.cumsum(onehot, axis=0)
    counts = csum[-1]
    rank = jnp.sum((csum - onehot) * onehot, axis=1)
    tiles_per = (counts + TM - 1) // TM
    tile_end = jnp.cumsum(tiles_per)
    tile_start = tile_end - tiles_per
    pos = tile_start[flat] * TM + rank
    token = jnp.tile(jnp.arange(L, dtype=jnp.int32), TOP_K)
    row_token = jnp.zeros((tmax * TM,), jnp.int32).at[pos].set(token)
    tile_ids = jnp.arange(tmax, dtype=jnp.int32)
    tile_expert = jnp.minimum(jnp.sum((tile_ids[:, None] >= tile_end[None, :]).astype(jnp.int32), axis=1),
                              N_EXPERTS - 1).astype(jnp.int32)
    n_tiles = tile_end[-1:].astype(jnp.int32)
    pos_tiles = pos.reshape(TOP_K, L // TC, TC).transpose(1, 0, 2).reshape(L // TC, 1, TOP_K * TC)
    return row_token.reshape(tmax, 1, TM), pos_tiles.astype(jnp.int32), tile_expert, n_tiles


def _moe_block(x, eid, gates, w_gate, w_up, w_down, ln_g, ln_b, alpha):
    L = x.shape[0]
    TM = min(256, L)
    TC = min(128, L)
    row_token, pos, tile_expert, n_tiles = _routing_tables(eid, TM, TC)
    act = moe_up(x, w_gate, w_up, tile_expert, n_tiles, row_token, TM)
    y = moe_down(act, w_down, tile_expert, n_tiles, TM)
    return moe_combine_ln(y, pos, gates.T, x, ln_g, ln_b, alpha, TC)


def _ab_mixer(xb, w_in_all, w_out_all, li, sink, pe, w1, b1, w2, b2):
    L, d = xb.shape
    ha = d // (2 * HEAD_DIM)
    hb = d // (2 * HEAD_DIM)
    aq, akv, bq = ha * HEAD_DIM, A_KV_HEADS * HEAD_DIM, hb * HEAD_DIM
    bkv = N_BRANCH * 2 * HEAD_DIM
    ngate = hb * N_BRANCH
    s0, s2, s3, s4 = aq, aq + 2 * akv, aq + 2 * akv + bq, aq + 2 * akv + bq + bkv
    chunks = ((0, s0, Q_SCALE), (s2, bq, Q_SCALE), (s0, 2 * akv, 1.0), (s3, bkv, 1.0), (s4, ngate, 1.0))
    w = cast_weight(w_in_all, li, chunks, pad_to=512)
    proj = proj_matmul(xb, w, BF16)
    qa0, qb0 = 0, ha
    ka0 = ha + hb
    va0 = ka0 + A_KV_HEADS
    kvb0 = va0 + A_KV_HEADS
    gate_blk = kvb0 + 2 * N_BRANCH
    slopes_a = _alibi_slopes2(ha)
    slopes_b = _alibi_slopes2(hb)
    oa = banded_attention(proj, slopes_a, sink.astype(F32) * LOG2E, q_base=qa0, k_blk=ka0, v_blk=va0,
                          n_kv=A_KV_HEADS, G=ha // A_KV_HEADS, window=A_WINDOW, has_sink=True)
    ng = L // CMP_STRIDE
    n_cmp = (L - CMP_LEN) // CMP_STRIDE + 1
    n_sel = L // SEL_BLOCK
    nsp = -(-n_sel // LANES) * LANES
    groups = proj[kvb0:kvb0 + 2].reshape(2, ng, CMP_STRIDE * HEAD_DIM)
    kvc = nsa_compress(groups, pe.reshape(2, 1, CMP_LEN * HEAD_DIM).astype(BF16), w1.astype(BF16),
                       b1.reshape(2, 1, CMP_HIDDEN).astype(F32), w2.astype(BF16),
                       b2.reshape(2, 1, HEAD_DIM).astype(F32))
    cstart = np.arange(ng) * CMP_STRIDE
    sstart = np.arange(nsp) * SEL_BLOCK
    overlap = ((cstart[:, None] < sstart[None, :] + SEL_BLOCK) & (cstart[:, None] + CMP_LEN > sstart[None, :])
               & (np.arange(ng)[:, None] < n_cmp) & (np.arange(nsp)[None, :] < n_sel))
    overlap = jnp.asarray(overlap.astype(np.float32), dtype=BF16)
    ob, sel = nsa_cmp_attention(proj, kvc, slopes_b, overlap, q_base=qb0, H=hb, gate_blk=gate_blk,
                                n_cmp=n_cmp, n_sel=n_sel)
    kcols, qx = _sel_position_columns(L, slopes_b)
    kx = jnp.concatenate([proj[kvb0 + 2], kcols], axis=1)
    vx = jnp.concatenate([proj[kvb0 + 3], jnp.ones((L, LANES), BF16)], axis=1)
    ob = nsa_sel_attention(proj, kx, vx, qx, sel, ob, q_base=qb0, H=hb, gate_blk=gate_blk)
    ob = banded_attention(proj, slopes_b, jnp.zeros((hb,), F32), q_base=qb0, k_blk=kvb0 + 4, v_blk=kvb0 + 5,
                          n_kv=1, G=hb, window=B_WINDOW, gate_blk=gate_blk, gate_col=2, oacc=ob)
    o = jnp.concatenate([oa, ob], axis=0)
    return outproj(o, cast_weight(w_out_all, li))


def _c_mixer(xb, w_in_all, w_out_all, li):
    d = xb.shape[1]
    hc = d // HEAD_DIM
    w = cast_weight(w_in_all, li, ((0, d, Q_SCALE), (d, 2 * d, 1.0)))
    qkv = proj_matmul(xb, w, F32)
    o = dilated_attention(qkv, _alibi_slopes2(hc), H=hc)
    return outproj(o, cast_weight(w_out_all, li))


def kernel(x, ab_w_in, ab_w_out, a_sink, nsa_cmp_pe, nsa_cmp_w1, nsa_cmp_b1, nsa_cmp_w2, nsa_cmp_b2, c_w_in, c_w_out, ln_mix_g, ln_mix_b, ln_ffn_g, ln_ffn_b, router_w, router_b, moe_w_gate, moe_w_up, moe_w_down):
    batch, L, d = x.shape
    depth = ln_mix_g.shape[0]
    alpha = float((2 * depth) ** 0.25)
    router_wt = router_w.T.astype(F32)
    router_bc = router_b.reshape(N_EXPERTS, 1).astype(F32)
    outs = []
    for bi in range(batch):
        xf = x[bi].astype(F32)
        xb = xf.astype(BF16)
        for layer in range(depth):
            i = layer // 2
            g_mix, b_mix = ln_mix_g[layer].reshape(1, d), ln_mix_b[layer].reshape(1, d)
            g_ffn, b_ffn = ln_ffn_g[layer].reshape(1, d), ln_ffn_b[layer].reshape(1, d)
            if layer % 2 == 0:
                mix = _ab_mixer(xb, ab_w_in, ab_w_out, i, a_sink[i], nsa_cmp_pe[i], nsa_cmp_w1[i],
                                nsa_cmp_b1[i], nsa_cmp_w2[i], nsa_cmp_b2[i])
            else:
                mix = _c_mixer(xb, c_w_in, c_w_out, i)
            xf, xb, eid, gates = ln_router(mix, xf, g_mix, b_mix, router_wt, router_bc, alpha)
            xf, xb = _moe_block(xf, eid, gates, moe_w_gate[layer].astype(BF16), moe_w_up[layer].astype(BF16),
                                moe_w_down[layer].astype(BF16), g_ffn, b_ffn, alpha)
        outs.append(xf)
    return jnp.stack(outs, axis=0).astype(x.dtype)
```

```python
import functools
import math

import numpy as np
import jax
import jax.numpy as jnp
from jax import lax
from jax.experimental import pallas as pl
from jax.experimental.pallas import tpu as pltpu

HEAD_DIM = 128
BLK = 128
A_KV_HEADS = 2
A_WINDOW = 128
N_BRANCH = 3
CMP_LEN = 32
CMP_STRIDE = 16
CMP_HIDDEN = 256
SEL_BLOCK = 64
SEL_TOPK = 16
B_WINDOW = 512
C_PATTERNS = ((128, 1), (512, 4), (2048, 16))
N_EXPERTS = 16
N_GROUPS = 4
EXPERTS_PER_GROUP = N_EXPERTS // N_GROUPS
TOP_K = 2
LN_EPS = 1e-5
NEG = -1e30
M_INIT = -1e20
LOG2E = math.log2(math.e)
Q_SCALE = HEAD_DIM ** -0.5 * LOG2E

LANES = 128
VMEM_BUDGET = 56 * 1024 * 1024

F32 = jnp.float32
BF16 = jnp.bfloat16
NT_DIMS = (((1,), (1,)), ((), ()))


def _params(n_grid, vmem_bytes):
    return pltpu.CompilerParams(
        dimension_semantics=("arbitrary",) * n_grid,
        vmem_limit_bytes=int(min(VMEM_BUDGET, max(vmem_bytes, 16 * 1024 * 1024))),
    )


def _alibi_slopes2(n):
    return jnp.asarray(2.0 ** (-8.0 * np.arange(1, n + 1) / n) * LOG2E, dtype=F32)


def _layer_norm_rows(z, g, b):
    mu = jnp.mean(z, axis=-1, keepdims=True)
    zc = z - mu
    var = jnp.mean(zc * zc, axis=-1, keepdims=True)
    return zc * lax.rsqrt(var + LN_EPS) * g + b


def _lane_tile(x, n):
    return x if n == LANES else jnp.concatenate([x] * (n // LANES), axis=1)


def _cast_kernel(w_ref, o_ref, *, chunks, n_used):
    dst = 0
    for src, size, scale in chunks:
        blk = w_ref[0, :, src:src + size]
        if scale != 1.0:
            blk = blk * scale
        o_ref[:, dst:dst + size] = blk.astype(o_ref.dtype)
        dst += size
    assert dst == n_used
    if n_used < o_ref.shape[1]:
        o_ref[:, n_used:] = jnp.zeros((o_ref.shape[0], o_ref.shape[1] - n_used), o_ref.dtype)


def cast_weight(w_stacked, layer, chunks=None, pad_to=1):
    _, k, n = w_stacked.shape
    if chunks is None:
        chunks = ((0, n, 1.0),)
    n_used = sum(c[1] for c in chunks)
    n_out = -(-n_used // pad_to) * pad_to
    tr = 128 if n > 8192 else 256
    tr = min(tr, k)
    kern = functools.partial(_cast_kernel, chunks=tuple(chunks), n_used=n_used)
    vmem = 2 * (tr * n * 4 + tr * n_out * 2) + tr * n * 4
    return pl.pallas_call(
        kern,
        grid=(k // tr,),
        in_specs=[pl.BlockSpec((1, tr, n), lambda r: (layer, r, 0))],
        out_specs=pl.BlockSpec((tr, n_out), lambda r: (r, 0)),
        out_shape=jax.ShapeDtypeStruct((k, n_out), BF16),
        compiler_params=_params(1, vmem + (8 << 20)),
        name="cast_weight",
    )(w_stacked)


def _proj_kernel(x_ref, w_ref, o_ref):
    acc = jnp.dot(x_ref[...], w_ref[...], preferred_element_type=F32)
    for c in range(o_ref.shape[0]):
        o_ref[c] = acc[:, c * LANES:(c + 1) * LANES].astype(o_ref.dtype)


def proj_matmul(x, w, out_dtype):
    m, k = x.shape
    n = w.shape[1]
    tm = min(1024, m)
    tn = min(512, n)
    osz = jnp.dtype(out_dtype).itemsize
    vmem = 2 * (tm * k * 2 + k * tn * 2 + tm * tn * osz) + 2 * tm * tn * 4
    return pl.pallas_call(
        _proj_kernel,
        grid=(m // tm, n // tn),
        in_specs=[pl.BlockSpec((tm, k), lambda i, j: (i, 0)),
                  pl.BlockSpec((k, tn), lambda i, j: (0, j))],
        out_specs=pl.BlockSpec((tn // LANES, tm, LANES), lambda i, j: (j, i, 0)),
        out_shape=jax.ShapeDtypeStruct((n // LANES, m, LANES), out_dtype),
        compiler_params=_params(2, vmem + (8 << 20)),
        name="proj_matmul",
    )(x, w)


def _banded_kernel(slopes_ref, sinks_ref, q_ref, kc_ref, kp_ref, vc_ref, vp_ref, *rest,
                   G, R, PR, window, has_sink, gate_col):
    if gate_col is not None:
        g_ref, oacc_ref, o_ref, kbuf, vbuf = rest
    else:
        o_ref, kbuf, vbuf = rest
    kv = pl.program_id(0)
    i = pl.program_id(1)
    kn = PR + BLK
    kbuf[0:PR, :] = kp_ref[0]
    kbuf[PR:PR + R, :] = kc_ref[0]
    vbuf[0:PR, :] = vp_ref[0]
    vbuf[PR:PR + R, :] = vc_ref[0]
    qi = lax.broadcasted_iota(jnp.int32, (BLK, kn), 0)
    kc = lax.broadcasted_iota(jnp.int32, (BLK, kn), 1)
    dist = PR + qi - kc
    static_valid = (dist >= 0) & (dist < window)
    negd = -dist.astype(F32)

    def unit(j, carry):
        row0 = pl.multiple_of(j * BLK, BLK)
        first_key = PR - (i * R + j * BLK)
        valid = static_valid & (kc >= first_key)
        k = kbuf[pl.ds(row0, kn), :]
        v = vbuf[pl.ds(row0, kn), :]
        q = q_ref[:, pl.ds(row0, BLK), :].reshape(G * BLK, HEAD_DIM)
        s_all = lax.dot_general(q, k, NT_DIMS, preferred_element_type=F32)
        ps, ls = [], []
        for g in range(G):
            h = kv * G + g
            s = s_all[g * BLK:(g + 1) * BLK] + slopes_ref[h] * negd
            s = jnp.where(valid, s, NEG)
            m = jnp.max(s, axis=-1, keepdims=True)
            p = jnp.exp2(s - m)
            l = jnp.sum(p, axis=-1, keepdims=True)
            if has_sink:
                l = l + jnp.exp2(sinks_ref[h] - m)
            ps.append(p.astype(BF16))
            ls.append(l)
        acc = jnp.dot(jnp.concatenate(ps, axis=0), v, preferred_element_type=F32)
        for g in range(G):
            o = acc[g * BLK:(g + 1) * BLK] / ls[g]
            if gate_col is not None:
                gl = g_ref[0, pl.ds(row0, BLK), :].astype(F32)
                c = g * N_BRANCH + gate_col
                o = oacc_ref[g, pl.ds(row0, BLK), :] + jax.nn.sigmoid(gl[:, c:c + 1]) * o
            o_ref[g, pl.ds(row0, BLK), :] = o.astype(o_ref.dtype)
        return carry

    lax.fori_loop(0, R // BLK, unit, 0)


def banded_attention(proj, slopes2, sinks2, *, q_base, k_blk, v_blk, n_kv, G, window,
                     has_sink=False, gate_blk=None, gate_col=None, oacc=None):
    L = proj.shape[1]
    n_prev = -(-(window - 1) // BLK)
    PR = n_prev * BLK
    R = max(PR, min(L, 8192 // G))
    assert L % R == 0 and R % PR == 0 and q_base % G == 0
    qb0 = q_base // G
    rp = R // PR
    in_specs = [
        pl.BlockSpec((G, R, LANES), lambda kv, i, *_: (qb0 + kv, i, 0)),
        pl.BlockSpec((1, R, LANES), lambda kv, i, *_: (k_blk + kv, i, 0)),
        pl.BlockSpec((1, PR, LANES), lambda kv, i, *_: (k_blk + kv, jnp.maximum(i * rp - 1, 0), 0)),
        pl.BlockSpec((1, R, LANES), lambda kv, i, *_: (v_blk + kv, i, 0)),
        pl.BlockSpec((1, PR, LANES), lambda kv, i, *_: (v_blk + kv, jnp.maximum(i * rp - 1, 0), 0)),
    ]
    args = [proj, proj, proj, proj, proj]
    vmem = 2 * (G * R * LANES * 2 * 2 + 4 * (R + PR) * LANES * 2) + 2 * (R + PR) * LANES * 2
    if gate_col is not None:
        assert n_kv == 1
        in_specs += [pl.BlockSpec((1, R, LANES), lambda kv, i, *_: (gate_blk, i, 0)),
                     pl.BlockSpec((G, R, LANES), lambda kv, i, *_: (kv, i, 0))]
        args += [proj, oacc]
        vmem += 2 * (R * LANES * 2 + G * R * LANES * 4)
    kern = functools.partial(_banded_kernel, G=G, R=R, PR=PR, window=window,
                             has_sink=has_sink, gate_col=gate_col)
    return pl.pallas_call(
        kern,
        grid_spec=pltpu.PrefetchScalarGridSpec(
            num_scalar_prefetch=2,
            grid=(n_kv, L // R),
            in_specs=in_specs,
            out_specs=pl.BlockSpec((G, R, LANES), lambda kv, i, *_: (kv, i, 0)),
            scratch_shapes=[pltpu.VMEM((PR + R, LANES), BF16), pltpu.VMEM((PR + R, LANES), BF16)],
        ),
        out_shape=jax.ShapeDtypeStruct((n_kv * G, L, LANES), BF16),
        compiler_params=_params(2, vmem + (16 << 20)),
        name="banded_attention",
    )(slopes2, sinks2, *args)


def _compress_kernel(g_ref, pe_ref, w1_ref, b1_ref, w2_ref, b2_ref, o_ref):
    ng = g_ref.shape[1]
    half = w1_ref.shape[1] // 2
    g = g_ref[0]
    y0 = jnp.dot(g, w1_ref[0, 0:half, :], preferred_element_type=F32)
    y1 = jnp.dot(g, w1_ref[0, half:, :], preferred_element_type=F32)
    y1 = pltpu.roll(y1, ng - 1, 0)
    pe = jnp.broadcast_to(pe_ref[0], (8, pe_ref.shape[2]))
    ype = jnp.dot(pe, w1_ref[0], preferred_element_type=F32)[0:1]
    h = jax.nn.gelu(y0 + y1 + ype + b1_ref[0])
    out = jnp.dot(h.astype(BF16), w2_ref[0], preferred_element_type=F32) + b2_ref[0]
    o_ref[0] = out.astype(o_ref.dtype)


def nsa_compress(groups, pe, w1, b1, w2, b2):
    _, ng, gw = groups.shape
    hid = w1.shape[2]
    vmem = 2 * (ng * gw * 2 + 2 * gw * hid * 2 + 2 * gw * 2) + 6 * ng * hid * 4
    return pl.pallas_call(
        _compress_kernel,
        grid=(2,),
        in_specs=[pl.BlockSpec((1, ng, gw), lambda c: (c, 0, 0)),
                  pl.BlockSpec((1, 1, 2 * gw), lambda c: (c, 0, 0)),
                  pl.BlockSpec((1, 2 * gw, hid), lambda c: (c, 0, 0)),
                  pl.BlockSpec((1, 1, hid), lambda c: (c, 0, 0)),
                  pl.BlockSpec((1, hid, HEAD_DIM), lambda c: (c, 0, 0)),
                  pl.BlockSpec((1, 1, HEAD_DIM), lambda c: (c, 0, 0))],
        out_specs=pl.BlockSpec((1, ng, HEAD_DIM), lambda c: (c, 0, 0)),
        out_shape=jax.ShapeDtypeStruct((2, ng, HEAD_DIM), BF16),
        compiler_params=_params(1, vmem + (8 << 20)),
        name="nsa_compress",
    )(groups, pe, w1, b1, w2, b2)


def _cmp_kernel(slopes_ref, q_ref, kv_ref, g_ref, ov_ref, o_ref, sel_ref, *, H, TQ, n_cmp, n_sel):
    qb = pl.program_id(0)
    ncp = kv_ref.shape[1]
    nsp = sel_ref.shape[1]
    kc = kv_ref[0]
    vc = kv_ref[1]
    t = qb * TQ + lax.broadcasted_iota(jnp.int32, (TQ, 1), 0)
    n = lax.broadcasted_iota(jnp.int32, (1, ncp), 1)
    cdist = t - (n * CMP_STRIDE + (CMP_LEN - 1))
    valid = (cdist >= 0) & (n < n_cmp)
    negd = -jnp.maximum(cdist, 0).astype(F32)
    gl = g_ref[0].astype(F32)
    psum = jnp.zeros((TQ, ncp), F32)
    for h in range(H):
        s = lax.dot_general(q_ref[h], kc, NT_DIMS, preferred_element_type=F32)
        s = jnp.where(valid, s + slopes_ref[h] * negd, NEG)
        m = jnp.max(s, axis=-1, keepdims=True)
        e = jnp.exp2(s - m)
        l = jnp.sum(e, axis=-1, keepdims=True)
        p = jnp.where(valid, e / l, 0.0)
        psum = psum + p
        o = jnp.dot(p.astype(BF16), vc, preferred_element_type=F32)
        c = h * N_BRANCH
        o_ref[h] = jax.nn.sigmoid(gl[:, c:c + 1]) * o
    p_hi = psum.astype(BF16)
    p_lo = (psum - p_hi.astype(F32)).astype(BF16)
    imp = (jnp.dot(p_hi, ov_ref[...], preferred_element_type=F32)
           + jnp.dot(p_lo, ov_ref[...], preferred_element_type=F32))
    j = lax.broadcasted_iota(jnp.int32, (1, nsp), 1)
    jf = j.astype(F32)
    cur = t // SEL_BLOCK
    forced = (j == 0) | (j == cur) | (j == cur - 1)
    svalid = j * SEL_BLOCK <= t
    work = jnp.where(forced, 1e30, jnp.where(svalid, imp, -1.0))
    work = jnp.where(j < n_sel, work, -2.0)
    sel = jnp.zeros((TQ, nsp), F32)
    for _ in range(min(SEL_TOPK, n_sel)):
        mx = jnp.max(work, axis=-1, keepdims=True)
        idx = jnp.min(jnp.where(work == mx, jf, 1e9), axis=-1, keepdims=True)
        hit = jf == idx
        sel = jnp.where(hit, 1.0, sel)
        work = jnp.where(hit, -2.0, work)
    sel_ref[...] = sel.astype(sel_ref.dtype)


def nsa_cmp_attention(proj, kvc, slopes2, overlap, *, q_base, H, gate_blk, n_cmp, n_sel):
    L = proj.shape[1]
    ncp = kvc.shape[1]
    nsp = overlap.shape[1]
    TQ = min(256, L)
    kern = functools.partial(_cmp_kernel, H=H, TQ=TQ, n_cmp=n_cmp, n_sel=n_sel)
    vmem = 2 * (H * TQ * LANES * 2 + 2 * ncp * LANES * 2 + TQ * LANES * 2 + ncp * nsp * 2
                + H * TQ * LANES * 4 + TQ * nsp * 2) + 8 * TQ * ncp * 4
    return pl.pallas_call(
        kern,
        grid_spec=pltpu.PrefetchScalarGridSpec(
            num_scalar_prefetch=1,
            grid=(L // TQ,),
            in_specs=[pl.BlockSpec((H, TQ, LANES), lambda qb, *_: (q_base // H, qb, 0)),
                      pl.BlockSpec((2, ncp, LANES), lambda qb, *_: (0, 0, 0)),
                      pl.BlockSpec((1, TQ, LANES), lambda qb, *_: (gate_blk, qb, 0)),
                      pl.BlockSpec((ncp, nsp), lambda qb, *_: (0, 0))],
            out_specs=[pl.BlockSpec((H, TQ, LANES), lambda qb, *_: (0, qb, 0)),
                       pl.BlockSpec((TQ, nsp), lambda qb, *_: (qb, 0))],
        ),
        out_shape=[jax.ShapeDtypeStruct((H, L, LANES), F32),
                   jax.ShapeDtypeStruct((L, nsp), BF16)],
        compiler_params=_params(1, vmem + (8 << 20)),
        name="nsa_cmp_attention",
    )(slopes2, proj, kvc, proj, overlap)


def _sel_kernel(q_ref, qx_ref, k_ref, v_ref, unsel_ref, g_ref, oacc_ref, o_ref, qs, m_s, acc_s, *, H, TQ, TK):
    qb = pl.program_id(0)
    kb = pl.program_id(1)
    nkb = pl.num_programs(1)

    @pl.when(kb == 0)
    def _():
        for h in range(H):
            qs[h * TQ:(h + 1) * TQ, 0:LANES] = q_ref[h]
        m_s[...] = jnp.full(m_s.shape, M_INIT, F32)
        acc_s[...] = jnp.zeros(acc_s.shape, F32)

    def step(causal):
        for h in range(H):
            qs[h * TQ:(h + 1) * TQ, LANES:2 * LANES] = qx_ref[h] + unsel_ref[0]
        s_all = lax.dot_general(qs[...], k_ref[...], NT_DIMS, preferred_element_type=F32)
        if causal:
            t = qb * TQ + lax.broadcasted_iota(jnp.int32, (TQ, 1), 0)
            kk = kb * TK + lax.broadcasted_iota(jnp.int32, (1, TK), 1)
            cmask = kk <= t
        ps = []
        for h in range(H):
            rows = slice(h * TQ, (h + 1) * TQ)
            s = s_all[rows]
            if causal:
                s = jnp.where(cmask, s, NEG)
            m_old = m_s[rows, :]
            m_new = jnp.maximum(m_old, jnp.max(s, axis=-1, keepdims=True))
            m_s[rows, :] = m_new
            alpha = jnp.exp2(m_old - m_new)
            acc_s[rows, :] = acc_s[rows, :] * _lane_tile(alpha, 2 * LANES)
            ps.append(jnp.exp2(s - _lane_tile(m_new, TK)).astype(BF16))
        acc_s[...] += jnp.dot(jnp.concatenate(ps, axis=0), v_ref[...], preferred_element_type=F32)

    last_key = kb * TK + (TK - 1)

    @pl.when(last_key <= qb * TQ)
    def _():
        step(False)

    @pl.when((last_key > qb * TQ) & (kb * TK <= qb * TQ + (TQ - 1)))
    def _():
        step(True)

    @pl.when(kb == nkb - 1)
    def _():
        gl = g_ref[0].astype(F32)
        for h in range(H):
            rows = slice(h * TQ, (h + 1) * TQ)
            c = h * N_BRANCH + 1
            o = acc_s[rows, 0:LANES] / acc_s[rows, LANES:2 * LANES]
            o_ref[h] = oacc_ref[h] + jax.nn.sigmoid(gl[:, c:c + 1]) * o


SEL_TQ = 128
SEL_TK = 1024
N_POS_COLS = 6
UNSEL_LOGIT = -(2.0 ** 100)


def nsa_sel_attention(proj, kx, vx, qx, unsel, oacc, *, q_base, H, gate_blk):
    L = proj.shape[1]
    TQ = min(SEL_TQ, L)
    TK = min(SEL_TK, L)

    def last_kb(qb):
        return (qb * TQ + TQ - 1) // TK

    kern = functools.partial(_sel_kernel, H=H, TQ=TQ, TK=TK)
    vmem = (2 * (H * TQ * LANES * 2 + 2 * TK * 2 * LANES * 2 + 2 * TQ * LANES * 2
                 + 2 * H * TQ * LANES * 4) + H * TQ * LANES * (4 + 4 + 8)
            + H * TQ * TK * (4 + 4 + 2))
    return pl.pallas_call(
        kern,
        grid=(L // TQ, L // TK),
        in_specs=[pl.BlockSpec((H, TQ, LANES), lambda qb, kb: (q_base // H, qb, 0)),
                  pl.BlockSpec((H, 1, LANES), lambda qb, kb: (0, 0, 0)),
                  pl.BlockSpec((TK, 2 * LANES), lambda qb, kb: (jnp.minimum(kb, last_kb(qb)), 0)),
                  pl.BlockSpec((TK, 2 * LANES), lambda qb, kb: (jnp.minimum(kb, last_kb(qb)), 0)),
                  pl.BlockSpec((1, TQ, LANES), lambda qb, kb: (jnp.minimum(kb, last_kb(qb)), qb, 0)),
                  pl.BlockSpec((1, TQ, LANES), lambda qb, kb: (gate_blk, qb, 0)),
                  pl.BlockSpec((H, TQ, LANES), lambda qb, kb: (0, qb, 0))],
        out_specs=pl.BlockSpec((H, TQ, LANES), lambda qb, kb: (0, qb, 0)),
        scratch_shapes=[pltpu.VMEM((H * TQ, 2 * LANES), BF16), pltpu.VMEM((H * TQ, LANES), F32),
                        pltpu.VMEM((H * TQ, 2 * LANES), F32)],
        out_shape=jax.ShapeDtypeStruct((H, L, LANES), F32),
        compiler_params=_params(2, vmem + (4 << 20)),
        name="nsa_sel_attention",
    )(proj, qx, kx, vx, unsel, proj, oacc)


def _sel_extra_columns(L, slopes2, sel, n_sel):
    TK = min(SEL_TK, L)
    nj = TK // SEL_BLOCK
    pos = np.arange(L)
    kcols = np.zeros((L, LANES), np.float32)
    kcols[:, 0:3] = ((pos // LANES) * LANES)[:, None]
    kcols[:, 3:6] = (pos % LANES)[:, None]
    kcols[pos, N_POS_COLS + (pos // SEL_BLOCK) % nj] = UNSEL_LOGIT
    a1 = slopes2.astype(BF16)
    r1 = slopes2 - a1.astype(F32)
    a2 = r1.astype(BF16)
    a3 = (r1 - a2.astype(F32)).astype(BF16)
    parts = jnp.stack([a1, a2, a3, a1, a2, a3], axis=1)
    qx = jnp.zeros((slopes2.shape[0], LANES), BF16).at[:, 0:N_POS_COLS].set(parts)
    flags = (1.0 - sel[:, :n_sel].astype(F32)).reshape(L, L // TK, nj).transpose(1, 0, 2)
    unsel = jnp.pad(flags, ((0, 0), (0, 0), (N_POS_COLS, LANES - N_POS_COLS - nj))).astype(BF16)
    return jnp.asarray(kcols, dtype=BF16), qx.reshape(-1, 1, LANES), unsel


def _dilated_kernel(slopes_ref, q_ref, k_ref, v_ref, o_ref, q4, k4, v4, ktail, vtail, o_s, l_s, *, CH):
    h = pl.program_id(0)
    i = pl.program_id(1)
    par = i % 2
    slope = slopes_ref[h]
    sub = CH // 4

    @pl.when(i == 0)
    def _():
        k4[1] = jnp.zeros(k4.shape[1:], F32)
        v4[1] = jnp.zeros(v4.shape[1:], F32)
        ktail[...] = jnp.zeros(ktail.shape, F32)
        vtail[...] = jnp.zeros(vtail.shape, F32)

    for r in range(4):
        q4[r] = q_ref[0, pl.ds(r, sub, stride=4), :]
        k4[par, r] = k_ref[0, pl.ds(r, sub, stride=4), :]
        v4[par, r] = v_ref[0, pl.ds(r, sub, stride=4), :]

    kn = 2 * BLK
    qi = lax.broadcasted_iota(jnp.int32, (BLK, kn), 0)
    kc = lax.broadcasted_iota(jnp.int32, (BLK, kn), 1)
    dist = BLK + qi - kc
    static_valid = (dist >= 0) & (dist <= BLK)
    first_valid = static_valid & (kc >= jnp.where(i == 0, BLK, 0))
    negd = -dist.astype(F32)
    ones = jnp.ones((kn, LANES), BF16)

    def unit(p, bias, q, kprev, kcur, vprev, vcur, rows):
        kk = jnp.concatenate([kprev.astype(BF16), kcur.astype(BF16)], axis=0)
        vv = jnp.concatenate([vprev.astype(BF16), vcur.astype(BF16)], axis=0)
        s = lax.dot_general(q.astype(BF16), kk, NT_DIMS, preferred_element_type=F32) + bias
        m = jnp.max(s, axis=-1, keepdims=True)
        e = jnp.exp2(s - m).astype(BF16)
        acc = jnp.dot(e, jnp.concatenate([vv, ones], axis=1), preferred_element_type=F32)
        l = acc[:, LANES:]
        o_s[p, rows, :] = acc[:, :LANES] / l
        l_s[p, rows, :] = m + jnp.log2(l)

    for p, (window, dil) in enumerate(C_PATTERNS):
        assert window // dil == BLK
        ab = (slope * dil) * negd
        bias = jnp.where(static_valid, ab, NEG)
        bias0 = jnp.where(first_valid, ab, NEG)
        if dil == 1:
            for j in range(CH // BLK):
                cur = slice(j * BLK, (j + 1) * BLK)
                prev = slice((j - 1) * BLK, j * BLK)
                unit(p, bias if j else bias0, q_ref[0, cur, :],
                     k_ref[0, prev, :] if j else ktail[...], k_ref[0, cur, :],
                     v_ref[0, prev, :] if j else vtail[...], v_ref[0, cur, :], pl.ds(j * BLK, BLK))
        elif dil == 4:
            for r in range(4):
                for j in range(sub // BLK):
                    cur = slice(j * BLK, (j + 1) * BLK)
                    prev = slice((j - 1) * BLK, j * BLK)
                    last = slice(sub - BLK, sub)
                    unit(p, bias if j else bias0, q4[r, cur, :],
                         k4[par, r, prev, :] if j else k4[1 - par, r, last, :], k4[par, r, cur, :],
                         v4[par, r, prev, :] if j else v4[1 - par, r, last, :], v4[par, r, cur, :],
                         pl.ds(r + 4 * j * BLK, BLK, stride=4))
        else:
            assert dil == 16 and sub == 4 * BLK
            for r in range(16):
                r4, a = r % 4, r // 4
                rows4 = pl.ds(a, BLK, stride=4)
                unit(p, bias0, q4[r4, rows4, :],
                     k4[1 - par, r4, rows4, :], k4[par, r4, rows4, :],
                     v4[1 - par, r4, rows4, :], v4[par, r4, rows4, :], pl.ds(r, BLK, stride=16))

    ktail[...] = k_ref[0, CH - BLK:CH, :]
    vtail[...] = v_ref[0, CH - BLK:CH, :]
    lses = [l_s[p] for p in range(len(C_PATTERNS))]
    mx = functools.reduce(jnp.maximum, lses)
    ws = [jnp.exp2(x - mx) for x in lses]
    num = functools.reduce(lambda a, b: a + b, [w * o_s[p] for p, w in enumerate(ws)])
    den = functools.reduce(lambda a, b: a + b, ws)
    o_ref[0] = (num / den).astype(o_ref.dtype)


def dilated_attention(qkv, slopes2, *, H):
    L = qkv.shape[1]
    CH = max(d for _, d in C_PATTERNS) * BLK
    assert L % CH == 0
    kern = functools.partial(_dilated_kernel, CH=CH)
    npat = len(C_PATTERNS)
    vmem = 2 * (3 * CH * LANES * 4 + CH * LANES * 2) + (5 + 2 * npat) * CH * LANES * 4
    return pl.pallas_call(
        kern,
        grid_spec=pltpu.PrefetchScalarGridSpec(
            num_scalar_prefetch=1,
            grid=(H, L // CH),
            in_specs=[pl.BlockSpec((1, CH, LANES), lambda h, i, *_: (h, i, 0)),
                      pl.BlockSpec((1, CH, LANES), lambda h, i, *_: (H + h, i, 0)),
                      pl.BlockSpec((1, CH, LANES), lambda h, i, *_: (2 * H + h, i, 0))],
            out_specs=pl.BlockSpec((1, CH, LANES), lambda h, i, *_: (h, i, 0)),
            scratch_shapes=[pltpu.VMEM((4, CH // 4, LANES), F32),
                            pltpu.VMEM((2, 4, CH // 4, LANES), F32),
                            pltpu.VMEM((2, 4, CH // 4, LANES), F32),
                            pltpu.VMEM((BLK, LANES), F32), pltpu.VMEM((BLK, LANES), F32),
                            pltpu.VMEM((npat, CH, LANES), F32), pltpu.VMEM((npat, CH, LANES), F32)],
        ),
        out_shape=jax.ShapeDtypeStruct((H, L, LANES), BF16),
        compiler_params=_params(2, vmem + (16 << 20)),
        name="dilated_attention",
    )(slopes2, qkv, qkv, qkv)


def _outproj_kernel(*refs):
    *o_refs, w_ref, y_ref = refs
    lhs = jnp.concatenate([o_ref[c] for o_ref in o_refs for c in range(o_ref.shape[0])], axis=1)
    y_ref[...] = jnp.dot(lhs, w_ref[...], preferred_element_type=F32)


def outproj(o_parts, w):
    m = o_parts[0].shape[1]
    k, n = w.shape
    assert k == sum(o.shape[0] for o in o_parts) * LANES
    tm = min(1024, m)
    tn = min(512, n)
    vmem = 2 * (tm * k * 2 + k * tn * 2 + tm * tn * 4) + tm * k * 2 + tm * tn * 4
    return pl.pallas_call(
        _outproj_kernel,
        grid=(m // tm, n // tn),
        in_specs=[pl.BlockSpec((o.shape[0], tm, LANES), lambda i, j: (0, i, 0)) for o in o_parts]
        + [pl.BlockSpec((k, tn), lambda i, j: (0, j))],
        out_specs=pl.BlockSpec((tm, tn), lambda i, j: (i, j)),
        out_shape=jax.ShapeDtypeStruct((m, n), F32),
        compiler_params=_params(2, vmem + (8 << 20)),
        name="outproj",
    )(*o_parts, w)


def _route(x, wt, b):
    logits = lax.dot_general(wt, x, NT_DIMS, preferred_element_type=F32,
                             precision=lax.Precision.HIGHEST) + b
    mx = jnp.max(logits, axis=0, keepdims=True)
    e = jnp.exp(logits - mx)
    probs = e / jnp.sum(e, axis=0, keepdims=True)
    rows = [probs[i:i + 1, :] for i in range(N_EXPERTS)]
    epg = EXPERTS_PER_GROUP
    best = None
    gsel = None
    for gi in range(N_GROUPS):
        grp = rows[gi * epg:(gi + 1) * epg]
        score = None
        for a in range(epg):
            for c in range(a + 1, epg):
                pair = grp[a] + grp[c]
                score = pair if score is None else jnp.maximum(score, pair)
        if best is None:
            best, gsel = score, jnp.zeros(score.shape, jnp.int32)
        else:
            better = score > best
            gsel = jnp.where(better, gi, gsel)
            best = jnp.maximum(best, score)
    ing = []
    for kk in range(epg):
        val = rows[kk]
        for gi in range(1, N_GROUPS):
            val = jnp.where(gsel == gi, rows[gi * epg + kk], val)
        ing.append(val)

    def argmax_first(vals):
        bv, bi = vals[0], jnp.zeros(vals[0].shape, jnp.int32)
        for kk in range(1, len(vals)):
            better = vals[kk] > bv
            bi = jnp.where(better, kk, bi)
            bv = jnp.maximum(bv, vals[kk])
        return bv, bi

    v1, i1 = argmax_first(ing)
    rest = [jnp.where(i1 == kk, -1.0, ing[kk]) for kk in range(epg)]
    v2, i2 = argmax_first(rest)
    tot = v1 + v2
    return (gsel * epg + i1, gsel * epg + i2), (v1 / tot, v2 / tot)


def _ln_router_kernel(y_ref, x_ref, g_ref, b_ref, wt_ref, rb_ref, xo_ref, xb_ref, eid_ref, gate_ref, *, alpha):
    xn = _layer_norm_rows(alpha * x_ref[...] + y_ref[...], g_ref[...], b_ref[...])
    xo_ref[...] = xn
    xb_ref[...] = xn.astype(BF16)
    eids, gates = _route(xn, wt_ref[...], rb_ref[...])
    for kk in range(TOP_K):
        eid_ref[kk:kk + 1, :] = eids[kk]
        gate_ref[kk:kk + 1, :] = gates[kk]


def ln_router(y, x, g, b, wt, rb, alpha):
    m, d = x.shape
    tr = min(256, m)
    kern = functools.partial(_ln_router_kernel, alpha=alpha)
    vmem = 2 * (3 * tr * d * 4 + tr * d * 2 + N_EXPERTS * d * 4) + 8 * tr * d * 2 + 3 * tr * d * 4
    return pl.pallas_call(
        kern,
        grid=(m // tr,),
        in_specs=[pl.BlockSpec((tr, d), lambda i: (i, 0)),
                  pl.BlockSpec((tr, d), lambda i: (i, 0)),
                  pl.BlockSpec((1, d), lambda i: (0, 0)),
                  pl.BlockSpec((1, d), lambda i: (0, 0)),
                  pl.BlockSpec((N_EXPERTS, d), lambda i: (0, 0)),
                  pl.BlockSpec((N_EXPERTS, 1), lambda i: (0, 0))],
        out_specs=[pl.BlockSpec((tr, d), lambda i: (i, 0)),
                   pl.BlockSpec((tr, d), lambda i: (i, 0)),
                   pl.BlockSpec((TOP_K, tr), lambda i: (0, i)),
                   pl.BlockSpec((TOP_K, tr), lambda i: (0, i))],
        out_shape=[jax.ShapeDtypeStruct((m, d), F32), jax.ShapeDtypeStruct((m, d), BF16),
                   jax.ShapeDtypeStruct((TOP_K, m), jnp.int32), jax.ShapeDtypeStruct((TOP_K, m), F32)],
        compiler_params=_params(1, vmem + (4 << 20)),
        name="ln_router",
    )(y, x, g, b, wt, rb)


def _row_copy(src_hbm, row, dst, slot, r, sem):
    return pltpu.make_async_copy(src_hbm.at[pl.ds(row, 1)], dst.at[slot, pl.ds(r, 1)], sem.at[slot])


def _gather_start(src_hbm, idx_ref, dst, slot, n_rows, sem):
    def body(r, c):
        _row_copy(src_hbm, idx_ref[0, 0, r], dst, slot, r, sem).start()
        return c
    lax.fori_loop(0, n_rows, body, 0, unroll=8)


def _gather_wait(src_hbm, dst, slot, n_rows, sem):
    pltpu.make_async_copy(src_hbm.at[pl.ds(0, n_rows)], dst.at[slot], sem.at[slot]).wait()


def _moe_up_kernel(te_ref, nt_ref, idx_ref, idx_next_ref, x_hbm, wg_ref, wu_ref, act_ref, xbuf, sem, *, TM):
    t = pl.program_id(0)
    nt = nt_ref[0]
    slot = t % 2

    @pl.when((t == 0) & (nt > 0))
    def _():
        _gather_start(x_hbm, idx_ref, xbuf, 0, TM, sem)

    @pl.when(t + 1 < nt)
    def _():
        _gather_start(x_hbm, idx_next_ref, xbuf, 1 - slot, TM, sem)

    @pl.when(t < nt)
    def _():
        _gather_wait(x_hbm, xbuf, slot, TM, sem)
        xb = xbuf[slot].astype(BF16)
        hg = jnp.dot(xb, wg_ref[0], preferred_element_type=F32)
        hu = jnp.dot(xb, wu_ref[0], preferred_element_type=F32)
        act_ref[...] = (jax.nn.silu(hg) * hu).astype(act_ref.dtype)

    @pl.when(t >= nt)
    def _():
        act_ref[...] = jnp.zeros(act_ref.shape, act_ref.dtype)


def moe_up(x, w_gate, w_up, e0, tile_expert, n_tiles, row_token, TM):
    d = x.shape[1]
    f = w_gate.shape[2]
    tmax = tile_expert.shape[0]
    kern = functools.partial(_moe_up_kernel, TM=TM)
    vmem = 2 * (2 * d * f * 2 + TM * f * 2) + 2 * TM * d * 4 + TM * d * 2 + 4 * TM * f * 4
    return pl.pallas_call(
        kern,
        grid_spec=pltpu.PrefetchScalarGridSpec(
            num_scalar_prefetch=2,
            grid=(tmax,),
            in_specs=[pl.BlockSpec((1, 1, TM), lambda t, te, nt: (t, 0, 0), memory_space=pltpu.SMEM),
                      pl.BlockSpec((1, 1, TM), lambda t, te, nt: (jnp.minimum(t + 1, tmax - 1), 0, 0),
                                   memory_space=pltpu.SMEM),
                      pl.BlockSpec(memory_space=pl.ANY),
                      pl.BlockSpec((1, d, f), lambda t, te, nt: (e0 + te[t], 0, 0)),
                      pl.BlockSpec((1, d, f), lambda t, te, nt: (e0 + te[t], 0, 0))],
            out_specs=pl.BlockSpec((TM, f), lambda t, te, nt: (t, 0)),
            scratch_shapes=[pltpu.VMEM((2, TM, d), F32), pltpu.SemaphoreType.DMA((2,))],
        ),
        out_shape=jax.ShapeDtypeStruct((tmax * TM, f), BF16),
        compiler_params=_params(1, vmem + (6 << 20)),
        name="moe_up",
    )(tile_expert, n_tiles, row_token, row_token, x, w_gate, w_up)


def _moe_down_kernel(te_ref, nt_ref, act_ref, wd_ref, y_ref):
    t = pl.program_id(0)

    @pl.when(t < nt_ref[0])
    def _():
        y_ref[...] = jnp.dot(act_ref[...], wd_ref[0], preferred_element_type=F32)

    @pl.when(t >= nt_ref[0])
    def _():
        y_ref[...] = jnp.zeros(y_ref.shape, y_ref.dtype)


def moe_down(act, w_down, e0, tile_expert, n_tiles, TM):
    f = act.shape[1]
    d = w_down.shape[2]
    tmax = tile_expert.shape[0]
    vmem = 2 * (TM * f * 2 + f * d * 2 + TM * d * 4) + TM * d * 4
    return pl.pallas_call(
        _moe_down_kernel,
        grid_spec=pltpu.PrefetchScalarGridSpec(
            num_scalar_prefetch=2,
            grid=(tmax,),
            in_specs=[pl.BlockSpec((TM, f), lambda t, te, nt: (t, 0)),
                      pl.BlockSpec((1, f, d), lambda t, te, nt: (e0 + te[t], 0, 0))],
            out_specs=pl.BlockSpec((TM, d), lambda t, te, nt: (t, 0)),
        ),
        out_shape=jax.ShapeDtypeStruct((tmax * TM, d), F32),
        compiler_params=_params(1, vmem + (8 << 20)),
        name="moe_down",
    )(tile_expert, n_tiles, act, w_down)


def _combine_ln_kernel(pos_ref, pos_next_ref, y_hbm, gate_ref, x_ref, g_ref, b_ref, xo_ref, xb_ref, ybuf, sem,
                       *, TC, alpha):
    i = pl.program_id(0)
    n = pl.num_programs(0)
    slot = i % 2
    rows = TOP_K * TC

    @pl.when(i == 0)
    def _():
        _gather_start(y_hbm, pos_ref, ybuf, 0, rows, sem)

    @pl.when(i + 1 < n)
    def _():
        _gather_start(y_hbm, pos_next_ref, ybuf, 1 - slot, rows, sem)

    _gather_wait(y_hbm, ybuf, slot, rows, sem)
    gate = gate_ref[...]
    ffn = gate[:, 0:1] * ybuf[slot, 0:TC, :]
    for kk in range(1, TOP_K):
        ffn = ffn + gate[:, kk:kk + 1] * ybuf[slot, kk * TC:(kk + 1) * TC, :]
    y = _layer_norm_rows(alpha * x_ref[...] + ffn, g_ref[...], b_ref[...])
    xo_ref[...] = y
    xb_ref[...] = y.astype(BF16)


def moe_combine_ln(y, pos, gates, x, g, b, alpha, TC):
    m, d = x.shape
    nt = m // TC
    kern = functools.partial(_combine_ln_kernel, TC=TC, alpha=alpha)
    vmem = 2 * (TC * LANES * 4 + 2 * TC * d * 4 + TC * d * 2) + 2 * TOP_K * TC * d * 4 + 3 * TC * d * 4
    return pl.pallas_call(
        kern,
        grid=(nt,),
        in_specs=[pl.BlockSpec((1, 1, TOP_K * TC), lambda i: (i, 0, 0), memory_space=pltpu.SMEM),
                  pl.BlockSpec((1, 1, TOP_K * TC), lambda i: (jnp.minimum(i + 1, nt - 1), 0, 0),
                               memory_space=pltpu.SMEM),
                  pl.BlockSpec(memory_space=pl.ANY),
                  pl.BlockSpec((TC, TOP_K), lambda i: (i, 0)),
                  pl.BlockSpec((TC, d), lambda i: (i, 0)),
                  pl.BlockSpec((1, d), lambda i: (0, 0)),
                  pl.BlockSpec((1, d), lambda i: (0, 0))],
        out_specs=[pl.BlockSpec((TC, d), lambda i: (i, 0)),
                   pl.BlockSpec((TC, d), lambda i: (i, 0))],
        out_shape=[jax.ShapeDtypeStruct((m, d), F32), jax.ShapeDtypeStruct((m, d), BF16)],
        scratch_shapes=[pltpu.VMEM((2, TOP_K * TC, d), F32), pltpu.SemaphoreType.DMA((2,))],
        compiler_params=_params(1, vmem + (8 << 20)),
        name="moe_combine_ln",
    )(pos, pos, y, gates, x, g, b)


def _routing_tables(eid, TM, TC):
    L = eid.shape[1]
    n_assign = TOP_K * L
    tmax = n_assign // TM + N_EXPERTS
    flat = eid.reshape(-1)
    onehot = (flat[:, None] == jnp.arange(N_EXPERTS, dtype=jnp.int32)[None, :]).astype(jnp.int32)
    csum = jnp.cumsum(onehot, axis=0)
    counts = csum[-1]
    rank = jnp.sum((csum - onehot) * onehot, axis=1)
    tiles_per = (counts + TM - 1) // TM
    tile_end = jnp.cumsum(tiles_per)
    tile_start = tile_end - tiles_per
    pos = tile_start[flat] * TM + rank
    token = jnp.tile(jnp.arange(L, dtype=jnp.int32), TOP_K)
    row_token = jnp.zeros((tmax * TM,), jnp.int32).at[pos].set(token)
    tile_ids = jnp.arange(tmax, dtype=jnp.int32)
    tile_expert = jnp.minimum(jnp.sum((tile_ids[:, None] >= tile_end[None, :]).astype(jnp.int32), axis=1),
                              N_EXPERTS - 1).astype(jnp.int32)
    n_tiles = tile_end[-1:].astype(jnp.int32)
    pos_tiles = pos.reshape(TOP_K, L // TC, TC).transpose(1, 0, 2).reshape(L // TC, 1, TOP_K * TC)
    return row_token.reshape(tmax, 1, TM), pos_tiles.astype(jnp.int32), tile_expert, n_tiles


def _moe_block(x, eid, gates, w_gate, w_up, w_down, e0, ln_g, ln_b, alpha):
    L = x.shape[0]
    TM = min(256, L)
    TC = min(128, L)
    row_token, pos, tile_expert, n_tiles = _routing_tables(eid, TM, TC)
    act = moe_up(x, w_gate, w_up, e0, tile_expert, n_tiles, row_token, TM)
    y = moe_down(act, w_down, e0, tile_expert, n_tiles, TM)
    return moe_combine_ln(y, pos, gates.T, x, ln_g, ln_b, alpha, TC)


def _ab_mixer(xb, w_in_all, w_out_all, li, sink, pe, w1, b1, w2, b2):
    L, d = xb.shape
    ha = d // (2 * HEAD_DIM)
    hb = d // (2 * HEAD_DIM)
    aq, akv, bq = ha * HEAD_DIM, A_KV_HEADS * HEAD_DIM, hb * HEAD_DIM
    bkv = N_BRANCH * 2 * HEAD_DIM
    ngate = hb * N_BRANCH
    s0, s2, s3, s4 = aq, aq + 2 * akv, aq + 2 * akv + bq, aq + 2 * akv + bq + bkv
    chunks = ((0, s0, Q_SCALE), (s2, bq, Q_SCALE), (s0, 2 * akv, 1.0), (s3, bkv, 1.0), (s4, ngate, 1.0))
    w = cast_weight(w_in_all, li, chunks, pad_to=512)
    proj = proj_matmul(xb, w, BF16)
    qa0, qb0 = 0, ha
    ka0 = ha + hb
    va0 = ka0 + A_KV_HEADS
    kvb0 = va0 + A_KV_HEADS
    gate_blk = kvb0 + 2 * N_BRANCH
    slopes_a = _alibi_slopes2(ha)
    slopes_b = _alibi_slopes2(hb)
    oa = banded_attention(proj, slopes_a, sink.astype(F32) * LOG2E, q_base=qa0, k_blk=ka0, v_blk=va0,
                          n_kv=A_KV_HEADS, G=ha // A_KV_HEADS, window=A_WINDOW, has_sink=True)
    ng = L // CMP_STRIDE
    n_cmp = (L - CMP_LEN) // CMP_STRIDE + 1
    n_sel = L // SEL_BLOCK
    nsp = -(-n_sel // LANES) * LANES
    groups = proj[kvb0:kvb0 + 2].reshape(2, ng, CMP_STRIDE * HEAD_DIM)
    kvc = nsa_compress(groups, pe.reshape(2, 1, CMP_LEN * HEAD_DIM).astype(BF16), w1.astype(BF16),
                       b1.reshape(2, 1, CMP_HIDDEN).astype(F32), w2.astype(BF16),
                       b2.reshape(2, 1, HEAD_DIM).astype(F32))
    cstart = np.arange(ng) * CMP_STRIDE
    sstart = np.arange(nsp) * SEL_BLOCK
    overlap = ((cstart[:, None] < sstart[None, :] + SEL_BLOCK) & (cstart[:, None] + CMP_LEN > sstart[None, :])
               & (np.arange(ng)[:, None] < n_cmp) & (np.arange(nsp)[None, :] < n_sel))
    overlap = jnp.asarray(overlap.astype(np.float32), dtype=BF16)
    ob, sel = nsa_cmp_attention(proj, kvc, slopes_b, overlap, q_base=qb0, H=hb, gate_blk=gate_blk,
                                n_cmp=n_cmp, n_sel=n_sel)
    kcols, qx, unsel = _sel_extra_columns(L, slopes_b, sel, n_sel)
    kx = jnp.concatenate([proj[kvb0 + 2], kcols], axis=1)
    vx = jnp.concatenate([proj[kvb0 + 3], jnp.ones((L, LANES), BF16)], axis=1)
    ob = nsa_sel_attention(proj, kx, vx, qx, unsel, ob, q_base=qb0, H=hb, gate_blk=gate_blk)
    ob = banded_attention(proj, slopes_b, jnp.zeros((hb,), F32), q_base=qb0, k_blk=kvb0 + 4, v_blk=kvb0 + 5,
                          n_kv=1, G=hb, window=B_WINDOW, gate_blk=gate_blk, gate_col=2, oacc=ob)
    return outproj([oa, ob], cast_weight(w_out_all, li))


def _c_mixer(xb, w_in_all, w_out_all, li):
    d = xb.shape[1]
    hc = d // HEAD_DIM
    w = cast_weight(w_in_all, li, ((0, d, Q_SCALE), (d, 2 * d, 1.0)))
    qkv = proj_matmul(xb, w, F32)
    o = dilated_attention(qkv, _alibi_slopes2(hc), H=hc)
    return outproj([o], cast_weight(w_out_all, li))


def kernel(x, ab_w_in, ab_w_out, a_sink, nsa_cmp_pe, nsa_cmp_w1, nsa_cmp_b1, nsa_cmp_w2, nsa_cmp_b2, c_w_in, c_w_out, ln_mix_g, ln_mix_b, ln_ffn_g, ln_ffn_b, router_w, router_b, moe_w_gate, moe_w_up, moe_w_down):
    batch, L, d = x.shape
    depth = ln_mix_g.shape[0]
    alpha = float((2 * depth) ** 0.25)
    router_wt = router_w.T.astype(F32)
    router_bc = router_b.reshape(N_EXPERTS, 1).astype(F32)
    d_ff = moe_w_gate.shape[-1]
    wg_all = moe_w_gate.astype(BF16).reshape(depth * N_EXPERTS, d, d_ff)
    wu_all = moe_w_up.astype(BF16).reshape(depth * N_EXPERTS, d, d_ff)
    wd_all = moe_w_down.astype(BF16).reshape(depth * N_EXPERTS, d_ff, d)
    outs = []
    for bi in range(batch):
        xf = x[bi].astype(F32)
        xb = xf.astype(BF16)
        for layer in range(depth):
            i = layer // 2
            g_mix, b_mix = ln_mix_g[layer].reshape(1, d), ln_mix_b[layer].reshape(1, d)
            g_ffn, b_ffn = ln_ffn_g[layer].reshape(1, d), ln_ffn_b[layer].reshape(1, d)
            if layer % 2 == 0:
                mix = _ab_mixer(xb, ab_w_in, ab_w_out, i, a_sink[i], nsa_cmp_pe[i], nsa_cmp_w1[i],
                                nsa_cmp_b1[i], nsa_cmp_w2[i], nsa_cmp_b2[i])
            else:
                mix = _c_mixer(xb, c_w_in, c_w_out, i)
            xf, xb, eid, gates = ln_router(mix, xf, g_mix, b_mix, router_wt, router_bc, alpha)
            xf, xb = _moe_block(xf, eid, gates, wg_all, wu_all, wd_all, layer * N_EXPERTS, g_ffn, b_ffn, alpha)
        outs.append(xf)
    return jnp.stack(outs, axis=0).astype(x.dtype)
```

```python
import functools
import math

import numpy as np
import jax
import jax.numpy as jnp
from jax import lax
from jax.experimental import pallas as pl
from jax.experimental.pallas import tpu as pltpu

HEAD_DIM = 128
BLK = 128
A_KV_HEADS = 2
A_WINDOW = 128
N_BRANCH = 3
CMP_LEN = 32
CMP_STRIDE = 16
CMP_HIDDEN = 256
SEL_BLOCK = 64
SEL_TOPK = 16
B_WINDOW = 512
C_PATTERNS = ((128, 1), (512, 4), (2048, 16))
N_EXPERTS = 16
N_GROUPS = 4
EXPERTS_PER_GROUP = N_EXPERTS // N_GROUPS
TOP_K = 2
LN_EPS = 1e-5
NEG = -1e30
M_INIT = -1e20
LOG2E = math.log2(math.e)
Q_SCALE = HEAD_DIM ** -0.5 * LOG2E

LANES = 128
VMEM_BUDGET = 56 * 1024 * 1024

F32 = jnp.float32
BF16 = jnp.bfloat16
NT_DIMS = (((1,), (1,)), ((), ()))


def _params(n_grid, vmem_bytes):
    return pltpu.CompilerParams(
        dimension_semantics=("arbitrary",) * n_grid,
        vmem_limit_bytes=int(min(VMEM_BUDGET, max(vmem_bytes, 16 * 1024 * 1024))),
    )


def _alibi_slopes2(n):
    return jnp.asarray(2.0 ** (-8.0 * np.arange(1, n + 1) / n) * LOG2E, dtype=F32)


def _layer_norm_rows(z, g, b):
    mu = jnp.mean(z, axis=-1, keepdims=True)
    zc = z - mu
    var = jnp.mean(zc * zc, axis=-1, keepdims=True)
    return zc * lax.rsqrt(var + LN_EPS) * g + b


def _lane_tile(x, n):
    return x if n == LANES else jnp.concatenate([x] * (n // LANES), axis=1)


def _cast_kernel(w_ref, o_ref, *, chunks, n_used):
    dst = 0
    for src, size, scale in chunks:
        blk = w_ref[0, :, src:src + size]
        if scale != 1.0:
            blk = blk * scale
        o_ref[:, dst:dst + size] = blk.astype(o_ref.dtype)
        dst += size
    assert dst == n_used
    if n_used < o_ref.shape[1]:
        o_ref[:, n_used:] = jnp.zeros((o_ref.shape[0], o_ref.shape[1] - n_used), o_ref.dtype)


def cast_weight(w_stacked, layer, chunks=None, pad_to=1):
    _, k, n = w_stacked.shape
    if chunks is None:
        chunks = ((0, n, 1.0),)
    n_used = sum(c[1] for c in chunks)
    n_out = -(-n_used // pad_to) * pad_to
    tr = 128 if n > 8192 else 256
    tr = min(tr, k)
    kern = functools.partial(_cast_kernel, chunks=tuple(chunks), n_used=n_used)
    vmem = 2 * (tr * n * 4 + tr * n_out * 2) + tr * n * 4
    return pl.pallas_call(
        kern,
        grid=(k // tr,),
        in_specs=[pl.BlockSpec((1, tr, n), lambda r: (layer, r, 0))],
        out_specs=pl.BlockSpec((tr, n_out), lambda r: (r, 0)),
        out_shape=jax.ShapeDtypeStruct((k, n_out), BF16),
        compiler_params=_params(1, vmem + (8 << 20)),
        name="cast_weight",
    )(w_stacked)


def _proj_kernel(x_ref, w_ref, o_ref):
    acc = jnp.dot(x_ref[...], w_ref[...], preferred_element_type=F32)
    for c in range(o_ref.shape[0]):
        o_ref[c] = acc[:, c * LANES:(c + 1) * LANES].astype(o_ref.dtype)


def proj_matmul(x, w, out_dtype):
    m, k = x.shape
    n = w.shape[1]
    tm = min(1024, m)
    tn = min(512, n)
    osz = jnp.dtype(out_dtype).itemsize
    vmem = 2 * (tm * k * 2 + k * tn * 2 + tm * tn * osz) + 2 * tm * tn * 4
    return pl.pallas_call(
        _proj_kernel,
        grid=(m // tm, n // tn),
        in_specs=[pl.BlockSpec((tm, k), lambda i, j: (i, 0)),
                  pl.BlockSpec((k, tn), lambda i, j: (0, j))],
        out_specs=pl.BlockSpec((tn // LANES, tm, LANES), lambda i, j: (j, i, 0)),
        out_shape=jax.ShapeDtypeStruct((n // LANES, m, LANES), out_dtype),
        compiler_params=_params(2, vmem + (8 << 20)),
        name="proj_matmul",
    )(x, w)


def _banded_kernel(slopes_ref, sinks_ref, q_ref, kc_ref, kp_ref, vc_ref, vp_ref, *rest,
                   G, R, PR, window, has_sink, gate_col):
    if gate_col is not None:
        g_ref, oacc_ref, o_ref, kbuf, vbuf = rest
    else:
        o_ref, kbuf, vbuf = rest
    kv = pl.program_id(0)
    i = pl.program_id(1)
    kn = PR + BLK
    kbuf[0:PR, :] = kp_ref[0]
    kbuf[PR:PR + R, :] = kc_ref[0]
    vbuf[0:PR, :] = vp_ref[0]
    vbuf[PR:PR + R, :] = vc_ref[0]
    qi = lax.broadcasted_iota(jnp.int32, (BLK, kn), 0)
    kc = lax.broadcasted_iota(jnp.int32, (BLK, kn), 1)
    dist = PR + qi - kc
    static_valid = (dist >= 0) & (dist < window)
    negd = -dist.astype(F32)

    def unit(j, carry):
        row0 = pl.multiple_of(j * BLK, BLK)
        first_key = PR - (i * R + j * BLK)
        valid = static_valid & (kc >= first_key)
        k = kbuf[pl.ds(row0, kn), :]
        v = vbuf[pl.ds(row0, kn), :]
        q = q_ref[:, pl.ds(row0, BLK), :].reshape(G * BLK, HEAD_DIM)
        s_all = lax.dot_general(q, k, NT_DIMS, preferred_element_type=F32)
        ps, ls = [], []
        for g in range(G):
            h = kv * G + g
            s = s_all[g * BLK:(g + 1) * BLK] + slopes_ref[h] * negd
            s = jnp.where(valid, s, NEG)
            m = jnp.max(s, axis=-1, keepdims=True)
            p = jnp.exp2(s - m)
            l = jnp.sum(p, axis=-1, keepdims=True)
            if has_sink:
                l = l + jnp.exp2(sinks_ref[h] - m)
            ps.append(p.astype(BF16))
            ls.append(l)
        acc = jnp.dot(jnp.concatenate(ps, axis=0), v, preferred_element_type=F32)
        for g in range(G):
            o = acc[g * BLK:(g + 1) * BLK] / ls[g]
            if gate_col is not None:
                gl = g_ref[0, pl.ds(row0, BLK), :].astype(F32)
                c = g * N_BRANCH + gate_col
                o = oacc_ref[g, pl.ds(row0, BLK), :] + jax.nn.sigmoid(gl[:, c:c + 1]) * o
            o_ref[g, pl.ds(row0, BLK), :] = o.astype(o_ref.dtype)
        return carry

    lax.fori_loop(0, R // BLK, unit, 0)


def banded_attention(proj, slopes2, sinks2, *, q_base, k_blk, v_blk, n_kv, G, window,
                     has_sink=False, gate_blk=None, gate_col=None, oacc=None):
    L = proj.shape[1]
    n_prev = -(-(window - 1) // BLK)
    PR = n_prev * BLK
    R = max(PR, min(L, 8192 // G))
    assert L % R == 0 and R % PR == 0 and q_base % G == 0
    qb0 = q_base // G
    rp = R // PR
    in_specs = [
        pl.BlockSpec((G, R, LANES), lambda kv, i, *_: (qb0 + kv, i, 0)),
        pl.BlockSpec((1, R, LANES), lambda kv, i, *_: (k_blk + kv, i, 0)),
        pl.BlockSpec((1, PR, LANES), lambda kv, i, *_: (k_blk + kv, jnp.maximum(i * rp - 1, 0), 0)),
        pl.BlockSpec((1, R, LANES), lambda kv, i, *_: (v_blk + kv, i, 0)),
        pl.BlockSpec((1, PR, LANES), lambda kv, i, *_: (v_blk + kv, jnp.maximum(i * rp - 1, 0), 0)),
    ]
    args = [proj, proj, proj, proj, proj]
    vmem = 2 * (G * R * LANES * 2 * 2 + 4 * (R + PR) * LANES * 2) + 2 * (R + PR) * LANES * 2
    if gate_col is not None:
        assert n_kv == 1
        in_specs += [pl.BlockSpec((1, R, LANES), lambda kv, i, *_: (gate_blk, i, 0)),
                     pl.BlockSpec((G, R, LANES), lambda kv, i, *_: (kv, i, 0))]
        args += [proj, oacc]
        vmem += 2 * (R * LANES * 2 + G * R * LANES * 4)
    kern = functools.partial(_banded_kernel, G=G, R=R, PR=PR, window=window,
                             has_sink=has_sink, gate_col=gate_col)
    return pl.pallas_call(
        kern,
        grid_spec=pltpu.PrefetchScalarGridSpec(
            num_scalar_prefetch=2,
            grid=(n_kv, L // R),
            in_specs=in_specs,
            out_specs=pl.BlockSpec((G, R, LANES), lambda kv, i, *_: (kv, i, 0)),
            scratch_shapes=[pltpu.VMEM((PR + R, LANES), BF16), pltpu.VMEM((PR + R, LANES), BF16)],
        ),
        out_shape=jax.ShapeDtypeStruct((n_kv * G, L, LANES), BF16),
        compiler_params=_params(2, vmem + (16 << 20)),
        name="banded_attention",
    )(slopes2, sinks2, *args)


def _compress_kernel(g_ref, pe_ref, w1_ref, b1_ref, w2_ref, b2_ref, o_ref):
    ng = g_ref.shape[1]
    half = w1_ref.shape[1] // 2
    g = g_ref[0]
    y0 = jnp.dot(g, w1_ref[0, 0:half, :], preferred_element_type=F32)
    y1 = jnp.dot(g, w1_ref[0, half:, :], preferred_element_type=F32)
    y1 = pltpu.roll(y1, ng - 1, 0)
    pe = jnp.broadcast_to(pe_ref[0], (8, pe_ref.shape[2]))
    ype = jnp.dot(pe, w1_ref[0], preferred_element_type=F32)[0:1]
    h = jax.nn.gelu(y0 + y1 + ype + b1_ref[0])
    out = jnp.dot(h.astype(BF16), w2_ref[0], preferred_element_type=F32) + b2_ref[0]
    o_ref[0] = out.astype(o_ref.dtype)


def nsa_compress(groups, pe, w1, b1, w2, b2):
    _, ng, gw = groups.shape
    hid = w1.shape[2]
    vmem = 2 * (ng * gw * 2 + 2 * gw * hid * 2 + 2 * gw * 2) + 6 * ng * hid * 4
    return pl.pallas_call(
        _compress_kernel,
        grid=(2,),
        in_specs=[pl.BlockSpec((1, ng, gw), lambda c: (c, 0, 0)),
                  pl.BlockSpec((1, 1, 2 * gw), lambda c: (c, 0, 0)),
                  pl.BlockSpec((1, 2 * gw, hid), lambda c: (c, 0, 0)),
                  pl.BlockSpec((1, 1, hid), lambda c: (c, 0, 0)),
                  pl.BlockSpec((1, hid, HEAD_DIM), lambda c: (c, 0, 0)),
                  pl.BlockSpec((1, 1, HEAD_DIM), lambda c: (c, 0, 0))],
        out_specs=pl.BlockSpec((1, ng, HEAD_DIM), lambda c: (c, 0, 0)),
        out_shape=jax.ShapeDtypeStruct((2, ng, HEAD_DIM), BF16),
        compiler_params=_params(1, vmem + (8 << 20)),
        name="nsa_compress",
    )(groups, pe, w1, b1, w2, b2)


def _cmp_kernel(slopes_ref, q_ref, kv_ref, g_ref, ov_ref, o_ref, sel_ref, *, H, TQ, n_cmp, n_sel):
    qb = pl.program_id(0)
    ncp = kv_ref.shape[1]
    nsp = sel_ref.shape[1]
    kc = kv_ref[0]
    vc = kv_ref[1]
    t = qb * TQ + lax.broadcasted_iota(jnp.int32, (TQ, 1), 0)
    n = lax.broadcasted_iota(jnp.int32, (1, ncp), 1)
    cdist = t - (n * CMP_STRIDE + (CMP_LEN - 1))
    valid = (cdist >= 0) & (n < n_cmp)
    negd = -jnp.maximum(cdist, 0).astype(F32)
    gl = g_ref[0].astype(F32)
    psum = jnp.zeros((TQ, ncp), F32)
    for h in range(H):
        s = lax.dot_general(q_ref[h], kc, NT_DIMS, preferred_element_type=F32)
        s = jnp.where(valid, s + slopes_ref[h] * negd, NEG)
        m = jnp.max(s, axis=-1, keepdims=True)
        e = jnp.exp2(s - m)
        l = jnp.sum(e, axis=-1, keepdims=True)
        p = jnp.where(valid, e / l, 0.0)
        psum = psum + p
        o = jnp.dot(p.astype(BF16), vc, preferred_element_type=F32)
        c = h * N_BRANCH
        o_ref[h] = jax.nn.sigmoid(gl[:, c:c + 1]) * o
    p_hi = psum.astype(BF16)
    p_lo = (psum - p_hi.astype(F32)).astype(BF16)
    imp = (jnp.dot(p_hi, ov_ref[...], preferred_element_type=F32)
           + jnp.dot(p_lo, ov_ref[...], preferred_element_type=F32))
    j = lax.broadcasted_iota(jnp.int32, (1, nsp), 1)
    jf = j.astype(F32)
    cur = t // SEL_BLOCK
    forced = (j == 0) | (j == cur) | (j == cur - 1)
    svalid = j * SEL_BLOCK <= t
    work = jnp.where(forced, 1e30, jnp.where(svalid, imp, -1.0))
    work = jnp.where(j < n_sel, work, -2.0)
    sel = jnp.zeros((TQ, nsp), F32)
    for _ in range(min(SEL_TOPK, n_sel)):
        mx = jnp.max(work, axis=-1, keepdims=True)
        idx = jnp.min(jnp.where(work == mx, jf, 1e9), axis=-1, keepdims=True)
        hit = jf == idx
        sel = jnp.where(hit, 1.0, sel)
        work = jnp.where(hit, -2.0, work)
    sel_ref[...] = sel.astype(sel_ref.dtype)


def nsa_cmp_attention(proj, kvc, slopes2, overlap, *, q_base, H, gate_blk, n_cmp, n_sel):
    L = proj.shape[1]
    ncp = kvc.shape[1]
    nsp = overlap.shape[1]
    TQ = min(256, L)
    kern = functools.partial(_cmp_kernel, H=H, TQ=TQ, n_cmp=n_cmp, n_sel=n_sel)
    vmem = 2 * (H * TQ * LANES * 2 + 2 * ncp * LANES * 2 + TQ * LANES * 2 + ncp * nsp * 2
                + H * TQ * LANES * 4 + TQ * nsp * 2) + 8 * TQ * ncp * 4
    return pl.pallas_call(
        kern,
        grid_spec=pltpu.PrefetchScalarGridSpec(
            num_scalar_prefetch=1,
            grid=(L // TQ,),
            in_specs=[pl.BlockSpec((H, TQ, LANES), lambda qb, *_: (q_base // H, qb, 0)),
                      pl.BlockSpec((2, ncp, LANES), lambda qb, *_: (0, 0, 0)),
                      pl.BlockSpec((1, TQ, LANES), lambda qb, *_: (gate_blk, qb, 0)),
                      pl.BlockSpec((ncp, nsp), lambda qb, *_: (0, 0))],
            out_specs=[pl.BlockSpec((H, TQ, LANES), lambda qb, *_: (0, qb, 0)),
                       pl.BlockSpec((TQ, nsp), lambda qb, *_: (qb, 0))],
        ),
        out_shape=[jax.ShapeDtypeStruct((H, L, LANES), F32),
                   jax.ShapeDtypeStruct((L, nsp), BF16)],
        compiler_params=_params(1, vmem + (8 << 20)),
        name="nsa_cmp_attention",
    )(slopes2, proj, kvc, proj, overlap)


def _sel_kernel(q_ref, qx_ref, k_ref, v_ref, unsel_ref, g_ref, oacc_ref, o_ref, qs, m_s, acc_s, *, H, TQ, TK):
    qb = pl.program_id(0)
    kb = pl.program_id(1)
    nkb = pl.num_programs(1)

    @pl.when(kb == 0)
    def _():
        for h in range(H):
            qs[h * TQ:(h + 1) * TQ, 0:LANES] = q_ref[h]
        m_s[...] = jnp.full(m_s.shape, M_INIT, F32)
        acc_s[...] = jnp.zeros(acc_s.shape, F32)

    def step(causal):
        for h in range(H):
            qs[h * TQ:(h + 1) * TQ, LANES:2 * LANES] = qx_ref[h] + unsel_ref[0]
        s_all = lax.dot_general(qs[...], k_ref[...], NT_DIMS, preferred_element_type=F32)
        if causal:
            t = qb * TQ + lax.broadcasted_iota(jnp.int32, (TQ, 1), 0)
            kk = kb * TK + lax.broadcasted_iota(jnp.int32, (1, TK), 1)
            cmask = kk <= t
        ps = []
        for h in range(H):
            rows = slice(h * TQ, (h + 1) * TQ)
            s = s_all[rows]
            if causal:
                s = jnp.where(cmask, s, NEG)
            m_old = m_s[rows, :]
            m_new = jnp.maximum(m_old, jnp.max(s, axis=-1, keepdims=True))
            m_s[rows, :] = m_new
            alpha = jnp.exp2(m_old - m_new)
            acc_s[rows, :] = acc_s[rows, :] * _lane_tile(alpha, 2 * LANES)
            ps.append(jnp.exp2(s - _lane_tile(m_new, TK)).astype(BF16))
        acc_s[...] += jnp.dot(jnp.concatenate(ps, axis=0), v_ref[...], preferred_element_type=F32)

    last_key = kb * TK + (TK - 1)

    @pl.when(last_key <= qb * TQ)
    def _():
        step(False)

    @pl.when((last_key > qb * TQ) & (kb * TK <= qb * TQ + (TQ - 1)))
    def _():
        step(True)

    @pl.when(kb == nkb - 1)
    def _():
        gl = g_ref[0].astype(F32)
        for h in range(H):
            rows = slice(h * TQ, (h + 1) * TQ)
            c = h * N_BRANCH + 1
            o = acc_s[rows, 0:LANES] / acc_s[rows, LANES:2 * LANES]
            o_ref[h] = oacc_ref[h] + jax.nn.sigmoid(gl[:, c:c + 1]) * o


SEL_TQ = 128
SEL_TK = 1024
N_POS_COLS = 6
UNSEL_LOGIT = -(2.0 ** 100)


def nsa_sel_attention(proj, kx, vx, qx, unsel, oacc, *, q_base, H, gate_blk):
    L = proj.shape[1]
    TQ = min(SEL_TQ, L)
    TK = min(SEL_TK, L)

    def last_kb(qb):
        return (qb * TQ + TQ - 1) // TK

    kern = functools.partial(_sel_kernel, H=H, TQ=TQ, TK=TK)
    vmem = (2 * (H * TQ * LANES * 2 + 2 * TK * 2 * LANES * 2 + 2 * TQ * LANES * 2
                 + 2 * H * TQ * LANES * 4) + H * TQ * LANES * (4 + 4 + 8)
            + H * TQ * TK * (4 + 4 + 2))
    return pl.pallas_call(
        kern,
        grid=(L // TQ, L // TK),
        in_specs=[pl.BlockSpec((H, TQ, LANES), lambda qb, kb: (q_base // H, qb, 0)),
                  pl.BlockSpec((H, 1, LANES), lambda qb, kb: (0, 0, 0)),
                  pl.BlockSpec((TK, 2 * LANES), lambda qb, kb: (jnp.minimum(kb, last_kb(qb)), 0)),
                  pl.BlockSpec((TK, 2 * LANES), lambda qb, kb: (jnp.minimum(kb, last_kb(qb)), 0)),
                  pl.BlockSpec((1, TQ, LANES), lambda qb, kb: (jnp.minimum(kb, last_kb(qb)), qb, 0)),
                  pl.BlockSpec((1, TQ, LANES), lambda qb, kb: (gate_blk, qb, 0)),
                  pl.BlockSpec((H, TQ, LANES), lambda qb, kb: (0, qb, 0))],
        out_specs=pl.BlockSpec((H, TQ, LANES), lambda qb, kb: (0, qb, 0)),
        scratch_shapes=[pltpu.VMEM((H * TQ, 2 * LANES), BF16), pltpu.VMEM((H * TQ, LANES), F32),
                        pltpu.VMEM((H * TQ, 2 * LANES), F32)],
        out_shape=jax.ShapeDtypeStruct((H, L, LANES), F32),
        compiler_params=_params(2, vmem + (4 << 20)),
        name="nsa_sel_attention",
    )(proj, qx, kx, vx, unsel, proj, oacc)


def _sel_extra_columns(L, slopes2, sel, n_sel):
    TK = min(SEL_TK, L)
    nj = TK // SEL_BLOCK
    pos = np.arange(L)
    kcols = np.zeros((L, LANES), np.float32)
    kcols[:, 0:3] = ((pos // LANES) * LANES)[:, None]
    kcols[:, 3:6] = (pos % LANES)[:, None]
    kcols[pos, N_POS_COLS + (pos // SEL_BLOCK) % nj] = UNSEL_LOGIT
    a1 = slopes2.astype(BF16)
    r1 = slopes2 - a1.astype(F32)
    a2 = r1.astype(BF16)
    a3 = (r1 - a2.astype(F32)).astype(BF16)
    parts = jnp.stack([a1, a2, a3, a1, a2, a3], axis=1)
    qx = jnp.zeros((slopes2.shape[0], LANES), BF16).at[:, 0:N_POS_COLS].set(parts)
    flags = (1.0 - sel[:, :n_sel].astype(F32)).reshape(L, L // TK, nj).transpose(1, 0, 2)
    unsel = jnp.pad(flags, ((0, 0), (0, 0), (N_POS_COLS, LANES - N_POS_COLS - nj))).astype(BF16)
    return jnp.asarray(kcols, dtype=BF16), qx.reshape(-1, 1, LANES), unsel


def _dilated_kernel(slopes_ref, q_ref, k_ref, v_ref, o_ref, q4, k4, v4, ktail, vtail, o_s, l_s, *, CH):
    h = pl.program_id(0)
    i = pl.program_id(1)
    par = i % 2
    slope = slopes_ref[h]
    sub = CH // 4

    @pl.when(i == 0)
    def _():
        k4[1] = jnp.zeros(k4.shape[1:], F32)
        v4[1] = jnp.zeros(v4.shape[1:], F32)
        ktail[...] = jnp.zeros(ktail.shape, F32)
        vtail[...] = jnp.zeros(vtail.shape, F32)

    for r in range(4):
        q4[r] = q_ref[0, pl.ds(r, sub, stride=4), :]
        k4[par, r] = k_ref[0, pl.ds(r, sub, stride=4), :]
        v4[par, r] = v_ref[0, pl.ds(r, sub, stride=4), :]

    kn = 2 * BLK
    qi = lax.broadcasted_iota(jnp.int32, (BLK, kn), 0)
    kc = lax.broadcasted_iota(jnp.int32, (BLK, kn), 1)
    dist = BLK + qi - kc
    static_valid = (dist >= 0) & (dist <= BLK)
    first_valid = static_valid & (kc >= jnp.where(i == 0, BLK, 0))
    negd = -dist.astype(F32)
    ones = jnp.ones((kn, LANES), BF16)

    def unit(p, bias, q, kprev, kcur, vprev, vcur, rows):
        kk = jnp.concatenate([kprev.astype(BF16), kcur.astype(BF16)], axis=0)
        vv = jnp.concatenate([vprev.astype(BF16), vcur.astype(BF16)], axis=0)
        s = lax.dot_general(q.astype(BF16), kk, NT_DIMS, preferred_element_type=F32) + bias
        m = jnp.max(s, axis=-1, keepdims=True)
        e = jnp.exp2(s - m).astype(BF16)
        acc = jnp.dot(e, jnp.concatenate([vv, ones], axis=1), preferred_element_type=F32)
        l = acc[:, LANES:]
        o_s[p, rows, :] = acc[:, :LANES] / l
        l_s[p, rows, :] = m + jnp.log2(l)

    for p, (window, dil) in enumerate(C_PATTERNS):
        assert window // dil == BLK
        ab = (slope * dil) * negd
        bias = jnp.where(static_valid, ab, NEG)
        bias0 = jnp.where(first_valid, ab, NEG)
        if dil == 1:
            for j in range(CH // BLK):
                cur = slice(j * BLK, (j + 1) * BLK)
                prev = slice((j - 1) * BLK, j * BLK)
                unit(p, bias if j else bias0, q_ref[0, cur, :],
                     k_ref[0, prev, :] if j else ktail[...], k_ref[0, cur, :],
                     v_ref[0, prev, :] if j else vtail[...], v_ref[0, cur, :], pl.ds(j * BLK, BLK))
        elif dil == 4:
            for r in range(4):
                for j in range(sub // BLK):
                    cur = slice(j * BLK, (j + 1) * BLK)
                    prev = slice((j - 1) * BLK, j * BLK)
                    last = slice(sub - BLK, sub)
                    unit(p, bias if j else bias0, q4[r, cur, :],
                         k4[par, r, prev, :] if j else k4[1 - par, r, last, :], k4[par, r, cur, :],
                         v4[par, r, prev, :] if j else v4[1 - par, r, last, :], v4[par, r, cur, :],
                         pl.ds(r + 4 * j * BLK, BLK, stride=4))
        else:
            assert dil == 16 and sub == 4 * BLK
            for r in range(16):
                r4, a = r % 4, r // 4
                rows4 = pl.ds(a, BLK, stride=4)
                unit(p, bias0, q4[r4, rows4, :],
                     k4[1 - par, r4, rows4, :], k4[par, r4, rows4, :],
                     v4[1 - par, r4, rows4, :], v4[par, r4, rows4, :], pl.ds(r, BLK, stride=16))

    ktail[...] = k_ref[0, CH - BLK:CH, :]
    vtail[...] = v_ref[0, CH - BLK:CH, :]
    lses = [l_s[p] for p in range(len(C_PATTERNS))]
    mx = functools.reduce(jnp.maximum, lses)
    ws = [jnp.exp2(x - mx) for x in lses]
    num = functools.reduce(lambda a, b: a + b, [w * o_s[p] for p, w in enumerate(ws)])
    den = functools.reduce(lambda a, b: a + b, ws)
    o_ref[0] = (num / den).astype(o_ref.dtype)


def dilated_attention(qkv, slopes2, *, H):
    L = qkv.shape[1]
    CH = max(d for _, d in C_PATTERNS) * BLK
    assert L % CH == 0
    kern = functools.partial(_dilated_kernel, CH=CH)
    npat = len(C_PATTERNS)
    vmem = 2 * (3 * CH * LANES * 4 + CH * LANES * 2) + (5 + 2 * npat) * CH * LANES * 4
    return pl.pallas_call(
        kern,
        grid_spec=pltpu.PrefetchScalarGridSpec(
            num_scalar_prefetch=1,
            grid=(H, L // CH),
            in_specs=[pl.BlockSpec((1, CH, LANES), lambda h, i, *_: (h, i, 0)),
                      pl.BlockSpec((1, CH, LANES), lambda h, i, *_: (H + h, i, 0)),
                      pl.BlockSpec((1, CH, LANES), lambda h, i, *_: (2 * H + h, i, 0))],
            out_specs=pl.BlockSpec((1, CH, LANES), lambda h, i, *_: (h, i, 0)),
            scratch_shapes=[pltpu.VMEM((4, CH // 4, LANES), F32),
                            pltpu.VMEM((2, 4, CH // 4, LANES), F32),
                            pltpu.VMEM((2, 4, CH // 4, LANES), F32),
                            pltpu.VMEM((BLK, LANES), F32), pltpu.VMEM((BLK, LANES), F32),
                            pltpu.VMEM((npat, CH, LANES), F32), pltpu.VMEM((npat, CH, LANES), F32)],
        ),
        out_shape=jax.ShapeDtypeStruct((H, L, LANES), BF16),
        compiler_params=_params(2, vmem + (16 << 20)),
        name="dilated_attention",
    )(slopes2, qkv, qkv, qkv)


def _outproj_kernel(*refs):
    *o_refs, w_ref, y_ref = refs
    lhs = jnp.concatenate([o_ref[c] for o_ref in o_refs for c in range(o_ref.shape[0])], axis=1)
    y_ref[...] = jnp.dot(lhs, w_ref[...], preferred_element_type=F32)


def outproj(o_parts, w):
    m = o_parts[0].shape[1]
    k, n = w.shape
    assert k == sum(o.shape[0] for o in o_parts) * LANES
    tm = min(1024, m)
    tn = min(512, n)
    vmem = 2 * (tm * k * 2 + k * tn * 2 + tm * tn * 4) + tm * k * 2 + tm * tn * 4
    return pl.pallas_call(
        _outproj_kernel,
        grid=(m // tm, n // tn),
        in_specs=[pl.BlockSpec((o.shape[0], tm, LANES), lambda i, j: (0, i, 0)) for o in o_parts]
        + [pl.BlockSpec((k, tn), lambda i, j: (0, j))],
        out_specs=pl.BlockSpec((tm, tn), lambda i, j: (i, j)),
        out_shape=jax.ShapeDtypeStruct((m, n), F32),
        compiler_params=_params(2, vmem + (8 << 20)),
        name="outproj",
    )(*o_parts, w)


def _route(x, wt, b):
    logits = lax.dot_general(wt, x, NT_DIMS, preferred_element_type=F32,
                             precision=lax.Precision.HIGHEST) + b
    mx = jnp.max(logits, axis=0, keepdims=True)
    e = jnp.exp(logits - mx)
    probs = e / jnp.sum(e, axis=0, keepdims=True)
    rows = [probs[i:i + 1, :] for i in range(N_EXPERTS)]
    epg = EXPERTS_PER_GROUP
    best = None
    gsel = None
    for gi in range(N_GROUPS):
        grp = rows[gi * epg:(gi + 1) * epg]
        score = None
        for a in range(epg):
            for c in range(a + 1, epg):
                pair = grp[a] + grp[c]
                score = pair if score is None else jnp.maximum(score, pair)
        if best is None:
            best, gsel = score, jnp.zeros(score.shape, jnp.int32)
        else:
            better = score > best
            gsel = jnp.where(better, gi, gsel)
            best = jnp.maximum(best, score)
    ing = []
    for kk in range(epg):
        val = rows[kk]
        for gi in range(1, N_GROUPS):
            val = jnp.where(gsel == gi, rows[gi * epg + kk], val)
        ing.append(val)

    def argmax_first(vals):
        bv, bi = vals[0], jnp.zeros(vals[0].shape, jnp.int32)
        for kk in range(1, len(vals)):
            better = vals[kk] > bv
            bi = jnp.where(better, kk, bi)
            bv = jnp.maximum(bv, vals[kk])
        return bv, bi

    v1, i1 = argmax_first(ing)
    rest = [jnp.where(i1 == kk, -1.0, ing[kk]) for kk in range(epg)]
    v2, i2 = argmax_first(rest)
    tot = v1 + v2
    return (gsel * epg + i1, gsel * epg + i2), (v1 / tot, v2 / tot)


def _ln_router_kernel(y_ref, x_ref, g_ref, b_ref, wt_ref, rb_ref, xo_ref, xb_ref, eid_ref, gate_ref, *, alpha):
    xn = _layer_norm_rows(alpha * x_ref[...] + y_ref[...], g_ref[...], b_ref[...])
    xo_ref[...] = xn
    xb_ref[...] = xn.astype(BF16)
    eids, gates = _route(xn, wt_ref[...], rb_ref[...])
    for kk in range(TOP_K):
        eid_ref[kk:kk + 1, :] = eids[kk]
        gate_ref[kk:kk + 1, :] = gates[kk]


def ln_router(y, x, g, b, wt, rb, alpha):
    m, d = x.shape
    tr = min(256, m)
    kern = functools.partial(_ln_router_kernel, alpha=alpha)
    vmem = 2 * (3 * tr * d * 4 + tr * d * 2 + N_EXPERTS * d * 4) + 8 * tr * d * 2 + 3 * tr * d * 4
    return pl.pallas_call(
        kern,
        grid=(m // tr,),
        in_specs=[pl.BlockSpec((tr, d), lambda i: (i, 0)),
                  pl.BlockSpec((tr, d), lambda i: (i, 0)),
                  pl.BlockSpec((1, d), lambda i: (0, 0)),
                  pl.BlockSpec((1, d), lambda i: (0, 0)),
                  pl.BlockSpec((N_EXPERTS, d), lambda i: (0, 0)),
                  pl.BlockSpec((N_EXPERTS, 1), lambda i: (0, 0))],
        out_specs=[pl.BlockSpec((tr, d), lambda i: (i, 0)),
                   pl.BlockSpec((tr, d), lambda i: (i, 0)),
                   pl.BlockSpec((TOP_K, tr), lambda i: (0, i)),
                   pl.BlockSpec((TOP_K, tr), lambda i: (0, i))],
        out_shape=[jax.ShapeDtypeStruct((m, d), F32), jax.ShapeDtypeStruct((m, d), BF16),
                   jax.ShapeDtypeStruct((TOP_K, m), jnp.int32), jax.ShapeDtypeStruct((TOP_K, m), F32)],
        compiler_params=_params(1, vmem + (4 << 20)),
        name="ln_router",
    )(y, x, g, b, wt, rb)


def _row_copy(src_hbm, row, dst, slot, r, sem):
    return pltpu.make_async_copy(src_hbm.at[pl.ds(row, 1)], dst.at[slot, pl.ds(r, 1)], sem.at[slot])


def _gather_start(src_hbm, idx_ref, dst, slot, n_rows, sem):
    def body(r, c):
        _row_copy(src_hbm, idx_ref[0, 0, r], dst, slot, r, sem).start()
        return c
    lax.fori_loop(0, n_rows, body, 0, unroll=8)


def _gather_wait(src_hbm, dst, slot, n_rows, sem):
    pltpu.make_async_copy(src_hbm.at[pl.ds(0, n_rows)], dst.at[slot], sem.at[slot]).wait()


def _moe_gate_kernel(te_ref, nt_ref, idx_ref, idx_next_ref, x_hbm, wg_ref, g_ref, xs_ref, xbuf, sem, *, TM):
    t = pl.program_id(0)
    nt = nt_ref[0]
    slot = t % 2

    @pl.when((t == 0) & (nt > 0))
    def _():
        _gather_start(x_hbm, idx_ref, xbuf, 0, TM, sem)

    @pl.when(t + 1 < nt)
    def _():
        _gather_start(x_hbm, idx_next_ref, xbuf, 1 - slot, TM, sem)

    @pl.when(t < nt)
    def _():
        _gather_wait(x_hbm, xbuf, slot, TM, sem)
        xb = xbuf[slot].astype(BF16)
        xs_ref[...] = xb
        g_ref[...] = jax.nn.silu(jnp.dot(xb, wg_ref[0].astype(BF16), preferred_element_type=F32))

    @pl.when(t >= nt)
    def _():
        xs_ref[...] = jnp.zeros(xs_ref.shape, xs_ref.dtype)
        g_ref[...] = jnp.zeros(g_ref.shape, g_ref.dtype)


def moe_gate(x, w_gate, e0, tile_expert, n_tiles, row_token, TM):
    d = x.shape[1]
    f = w_gate.shape[2]
    tmax = tile_expert.shape[0]
    kern = functools.partial(_moe_gate_kernel, TM=TM)
    vmem = 2 * (d * f * 4 + TM * f * 4 + TM * d * 2) + d * f * 2 + 2 * TM * d * 4 + TM * d * 2 + 2 * TM * f * 4
    return pl.pallas_call(
        kern,
        grid_spec=pltpu.PrefetchScalarGridSpec(
            num_scalar_prefetch=2,
            grid=(tmax,),
            in_specs=[pl.BlockSpec((1, 1, TM), lambda t, te, nt: (t, 0, 0), memory_space=pltpu.SMEM),
                      pl.BlockSpec((1, 1, TM), lambda t, te, nt: (jnp.minimum(t + 1, tmax - 1), 0, 0),
                                   memory_space=pltpu.SMEM),
                      pl.BlockSpec(memory_space=pl.ANY),
                      pl.BlockSpec((1, d, f), lambda t, te, nt: (e0 + te[t], 0, 0))],
            out_specs=[pl.BlockSpec((TM, f), lambda t, te, nt: (t, 0)),
                       pl.BlockSpec((TM, d), lambda t, te, nt: (t, 0))],
            scratch_shapes=[pltpu.VMEM((2, TM, d), F32), pltpu.SemaphoreType.DMA((2,))],
        ),
        out_shape=[jax.ShapeDtypeStruct((tmax * TM, f), F32), jax.ShapeDtypeStruct((tmax * TM, d), BF16)],
        compiler_params=_params(1, vmem + (6 << 20)),
        name="moe_gate",
    )(tile_expert, n_tiles, row_token, row_token, x, w_gate)


def _moe_up_kernel(te_ref, nt_ref, xs_ref, g_ref, wu_ref, act_ref):
    t = pl.program_id(0)

    @pl.when(t < nt_ref[0])
    def _():
        hu = jnp.dot(xs_ref[...], wu_ref[0].astype(BF16), preferred_element_type=F32)
        act_ref[...] = (g_ref[...] * hu).astype(act_ref.dtype)

    @pl.when(t >= nt_ref[0])
    def _():
        act_ref[...] = jnp.zeros(act_ref.shape, act_ref.dtype)


def moe_up(xs, g, w_up, e0, tile_expert, n_tiles, TM):
    d = xs.shape[1]
    f = w_up.shape[2]
    tmax = tile_expert.shape[0]
    vmem = 2 * (d * f * 4 + TM * f * 4 + TM * d * 2 + TM * f * 2) + d * f * 2 + 2 * TM * f * 4
    return pl.pallas_call(
        _moe_up_kernel,
        grid_spec=pltpu.PrefetchScalarGridSpec(
            num_scalar_prefetch=2,
            grid=(tmax,),
            in_specs=[pl.BlockSpec((TM, d), lambda t, te, nt: (t, 0)),
                      pl.BlockSpec((TM, f), lambda t, te, nt: (t, 0)),
                      pl.BlockSpec((1, d, f), lambda t, te, nt: (e0 + te[t], 0, 0))],
            out_specs=pl.BlockSpec((TM, f), lambda t, te, nt: (t, 0)),
        ),
        out_shape=jax.ShapeDtypeStruct((tmax * TM, f), BF16),
        compiler_params=_params(1, vmem + (6 << 20)),
        name="moe_up",
    )(tile_expert, n_tiles, xs, g, w_up)


def _moe_down_kernel(te_ref, nt_ref, act_ref, wd_ref, y_ref):
    t = pl.program_id(0)

    @pl.when(t < nt_ref[0])
    def _():
        y_ref[...] = jnp.dot(act_ref[...], wd_ref[0].astype(BF16), preferred_element_type=F32)

    @pl.when(t >= nt_ref[0])
    def _():
        y_ref[...] = jnp.zeros(y_ref.shape, y_ref.dtype)


def moe_down(act, w_down, e0, tile_expert, n_tiles, TM):
    f = act.shape[1]
    d = w_down.shape[2]
    tmax = tile_expert.shape[0]
    vmem = 2 * (TM * f * 2 + f * d * 4 + TM * d * 4) + f * d * 2 + TM * d * 4
    return pl.pallas_call(
        _moe_down_kernel,
        grid_spec=pltpu.PrefetchScalarGridSpec(
            num_scalar_prefetch=2,
            grid=(tmax,),
            in_specs=[pl.BlockSpec((TM, f), lambda t, te, nt: (t, 0)),
                      pl.BlockSpec((1, f, d), lambda t, te, nt: (e0 + te[t], 0, 0))],
            out_specs=pl.BlockSpec((TM, d), lambda t, te, nt: (t, 0)),
        ),
        out_shape=jax.ShapeDtypeStruct((tmax * TM, d), F32),
        compiler_params=_params(1, vmem + (8 << 20)),
        name="moe_down",
    )(tile_expert, n_tiles, act, w_down)


def _combine_ln_kernel(pos_ref, pos_next_ref, y_hbm, gate_ref, x_ref, g_ref, b_ref, xo_ref, xb_ref, ybuf, sem,
                       *, TC, alpha):
    i = pl.program_id(0)
    n = pl.num_programs(0)
    slot = i % 2
    rows = TOP_K * TC

    @pl.when(i == 0)
    def _():
        _gather_start(y_hbm, pos_ref, ybuf, 0, rows, sem)

    @pl.when(i + 1 < n)
    def _():
        _gather_start(y_hbm, pos_next_ref, ybuf, 1 - slot, rows, sem)

    _gather_wait(y_hbm, ybuf, slot, rows, sem)
    gate = gate_ref[...]
    ffn = gate[:, 0:1] * ybuf[slot, 0:TC, :]
    for kk in range(1, TOP_K):
        ffn = ffn + gate[:, kk:kk + 1] * ybuf[slot, kk * TC:(kk + 1) * TC, :]
    y = _layer_norm_rows(alpha * x_ref[...] + ffn, g_ref[...], b_ref[...])
    xo_ref[...] = y
    xb_ref[...] = y.astype(BF16)


def moe_combine_ln(y, pos, gates, x, g, b, alpha, TC):
    m, d = x.shape
    nt = m // TC
    kern = functools.partial(_combine_ln_kernel, TC=TC, alpha=alpha)
    vmem = 2 * (TC * LANES * 4 + 2 * TC * d * 4 + TC * d * 2) + 2 * TOP_K * TC * d * 4 + 3 * TC * d * 4
    return pl.pallas_call(
        kern,
        grid=(nt,),
        in_specs=[pl.BlockSpec((1, 1, TOP_K * TC), lambda i: (i, 0, 0), memory_space=pltpu.SMEM),
                  pl.BlockSpec((1, 1, TOP_K * TC), lambda i: (jnp.minimum(i + 1, nt - 1), 0, 0),
                               memory_space=pltpu.SMEM),
                  pl.BlockSpec(memory_space=pl.ANY),
                  pl.BlockSpec((TC, TOP_K), lambda i: (i, 0)),
                  pl.BlockSpec((TC, d), lambda i: (i, 0)),
                  pl.BlockSpec((1, d), lambda i: (0, 0)),
                  pl.BlockSpec((1, d), lambda i: (0, 0))],
        out_specs=[pl.BlockSpec((TC, d), lambda i: (i, 0)),
                   pl.BlockSpec((TC, d), lambda i: (i, 0))],
        out_shape=[jax.ShapeDtypeStruct((m, d), F32), jax.ShapeDtypeStruct((m, d), BF16)],
        scratch_shapes=[pltpu.VMEM((2, TOP_K * TC, d), F32), pltpu.SemaphoreType.DMA((2,))],
        compiler_params=_params(1, vmem + (8 << 20)),
        name="moe_combine_ln",
    )(pos, pos, y, gates, x, g, b)


def _routing_tables(eid, TM, TC):
    L = eid.shape[1]
    n_assign = TOP_K * L
    tmax = n_assign // TM + N_EXPERTS
    flat = eid.reshape(-1)
    onehot = (flat[:, None] == jnp.arange(N_EXPERTS, dtype=jnp.int32)[None, :]).astype(jnp.int32)
    csum = jnp.cumsum(onehot, axis=0)
    counts = csum[-1]
    rank = jnp.sum((csum - onehot) * onehot, axis=1)
    tiles_per = (counts + TM - 1) // TM
    tile_end = jnp.cumsum(tiles_per)
    tile_start = tile_end - tiles_per
    pos = tile_start[flat] * TM + rank
    token = jnp.tile(jnp.arange(L, dtype=jnp.int32), TOP_K)
    row_token = jnp.zeros((tmax * TM,), jnp.int32).at[pos].set(token)
    tile_ids = jnp.arange(tmax, dtype=jnp.int32)
    tile_expert = jnp.minimum(jnp.sum((tile_ids[:, None] >= tile_end[None, :]).astype(jnp.int32), axis=1),
                              N_EXPERTS - 1).astype(jnp.int32)
    n_tiles = tile_end[-1:].astype(jnp.int32)
    tile_expert = jnp.where(tile_ids < n_tiles[0], tile_expert, tile_expert[jnp.maximum(n_tiles[0] - 1, 0)])
    pos_tiles = pos.reshape(TOP_K, L // TC, TC).transpose(1, 0, 2).reshape(L // TC, 1, TOP_K * TC)
    return row_token.reshape(tmax, 1, TM), pos_tiles.astype(jnp.int32), tile_expert, n_tiles


def _moe_block(x, eid, gates, w_gate, w_up, w_down, e0, ln_g, ln_b, alpha):
    L = x.shape[0]
    TM = min(256, L)
    TC = min(128, L)
    row_token, pos, tile_expert, n_tiles = _routing_tables(eid, TM, TC)
    g, xs = moe_gate(x, w_gate, e0, tile_expert, n_tiles, row_token, TM)
    act = moe_up(xs, g, w_up, e0, tile_expert, n_tiles, TM)
    y = moe_down(act, w_down, e0, tile_expert, n_tiles, TM)
    return moe_combine_ln(y, pos, gates.T, x, ln_g, ln_b, alpha, TC)


def _ab_mixer(xb, w_in_all, w_out_all, li, sink, pe, w1, b1, w2, b2):
    L, d = xb.shape
    ha = d // (2 * HEAD_DIM)
    hb = d // (2 * HEAD_DIM)
    aq, akv, bq = ha * HEAD_DIM, A_KV_HEADS * HEAD_DIM, hb * HEAD_DIM
    bkv = N_BRANCH * 2 * HEAD_DIM
    ngate = hb * N_BRANCH
    s0, s2, s3, s4 = aq, aq + 2 * akv, aq + 2 * akv + bq, aq + 2 * akv + bq + bkv
    chunks = ((0, s0, Q_SCALE), (s2, bq, Q_SCALE), (s0, 2 * akv, 1.0), (s3, bkv, 1.0), (s4, ngate, 1.0))
    w = cast_weight(w_in_all, li, chunks, pad_to=512)
    proj = proj_matmul(xb, w, BF16)
    qa0, qb0 = 0, ha
    ka0 = ha + hb
    va0 = ka0 + A_KV_HEADS
    kvb0 = va0 + A_KV_HEADS
    gate_blk = kvb0 + 2 * N_BRANCH
    slopes_a = _alibi_slopes2(ha)
    slopes_b = _alibi_slopes2(hb)
    oa = banded_attention(proj, slopes_a, sink.astype(F32) * LOG2E, q_base=qa0, k_blk=ka0, v_blk=va0,
                          n_kv=A_KV_HEADS, G=ha // A_KV_HEADS, window=A_WINDOW, has_sink=True)
    ng = L // CMP_STRIDE
    n_cmp = (L - CMP_LEN) // CMP_STRIDE + 1
    n_sel = L // SEL_BLOCK
    nsp = -(-n_sel // LANES) * LANES
    groups = proj[kvb0:kvb0 + 2].reshape(2, ng, CMP_STRIDE * HEAD_DIM)
    kvc = nsa_compress(groups, pe.reshape(2, 1, CMP_LEN * HEAD_DIM).astype(BF16), w1.astype(BF16),
                       b1.reshape(2, 1, CMP_HIDDEN).astype(F32), w2.astype(BF16),
                       b2.reshape(2, 1, HEAD_DIM).astype(F32))
    cstart = np.arange(ng) * CMP_STRIDE
    sstart = np.arange(nsp) * SEL_BLOCK
    overlap = ((cstart[:, None] < sstart[None, :] + SEL_BLOCK) & (cstart[:, None] + CMP_LEN > sstart[None, :])
               & (np.arange(ng)[:, None] < n_cmp) & (np.arange(nsp)[None, :] < n_sel))
    overlap = jnp.asarray(overlap.astype(np.float32), dtype=BF16)
    ob, sel = nsa_cmp_attention(proj, kvc, slopes_b, overlap, q_base=qb0, H=hb, gate_blk=gate_blk,
                                n_cmp=n_cmp, n_sel=n_sel)
    kcols, qx, unsel = _sel_extra_columns(L, slopes_b, sel, n_sel)
    kx = jnp.concatenate([proj[kvb0 + 2], kcols], axis=1)
    vx = jnp.concatenate([proj[kvb0 + 3], jnp.ones((L, LANES), BF16)], axis=1)
    ob = nsa_sel_attention(proj, kx, vx, qx, unsel, ob, q_base=qb0, H=hb, gate_blk=gate_blk)
    ob = banded_attention(proj, slopes_b, jnp.zeros((hb,), F32), q_base=qb0, k_blk=kvb0 + 4, v_blk=kvb0 + 5,
                          n_kv=1, G=hb, window=B_WINDOW, gate_blk=gate_blk, gate_col=2, oacc=ob)
    return outproj([oa, ob], cast_weight(w_out_all, li))


def _c_mixer(xb, w_in_all, w_out_all, li):
    d = xb.shape[1]
    hc = d // HEAD_DIM
    w = cast_weight(w_in_all, li, ((0, d, Q_SCALE), (d, 2 * d, 1.0)))
    qkv = proj_matmul(xb, w, F32)
    o = dilated_attention(qkv, _alibi_slopes2(hc), H=hc)
    return outproj([o], cast_weight(w_out_all, li))


def kernel(x, ab_w_in, ab_w_out, a_sink, nsa_cmp_pe, nsa_cmp_w1, nsa_cmp_b1, nsa_cmp_w2, nsa_cmp_b2, c_w_in, c_w_out, ln_mix_g, ln_mix_b, ln_ffn_g, ln_ffn_b, router_w, router_b, moe_w_gate, moe_w_up, moe_w_down):
    batch, L, d = x.shape
    depth = ln_mix_g.shape[0]
    alpha = float((2 * depth) ** 0.25)
    router_wt = router_w.T.astype(F32)
    router_bc = router_b.reshape(N_EXPERTS, 1).astype(F32)
    d_ff = moe_w_gate.shape[-1]
    wg_all = moe_w_gate.astype(F32).reshape(depth * N_EXPERTS, d, d_ff)
    wu_all = moe_w_up.astype(F32).reshape(depth * N_EXPERTS, d, d_ff)
    wd_all = moe_w_down.astype(F32).reshape(depth * N_EXPERTS, d_ff, d)
    outs = []
    for bi in range(batch):
        xf = x[bi].astype(F32)
        xb = xf.astype(BF16)
        for layer in range(depth):
            i = layer // 2
            g_mix, b_mix = ln_mix_g[layer].reshape(1, d), ln_mix_b[layer].reshape(1, d)
            g_ffn, b_ffn = ln_ffn_g[layer].reshape(1, d), ln_ffn_b[layer].reshape(1, d)
            if layer % 2 == 0:
                mix = _ab_mixer(xb, ab_w_in, ab_w_out, i, a_sink[i], nsa_cmp_pe[i], nsa_cmp_w1[i],
                                nsa_cmp_b1[i], nsa_cmp_w2[i], nsa_cmp_b2[i])
            else:
                mix = _c_mixer(xb, c_w_in, c_w_out, i)
            xf, xb, eid, gates = ln_router(mix, xf, g_mix, b_mix, router_wt, router_bc, alpha)
            xf, xb = _moe_block(xf, eid, gates, wg_all, wu_all, wd_all, layer * N_EXPERTS, g_ffn, b_ffn, alpha)
        outs.append(xf)
    return jnp.stack(outs, axis=0).astype(x.dtype)
```

```python
import functools
import math

import numpy as np
import jax
import jax.numpy as jnp
from jax import lax
from jax.experimental import pallas as pl
from jax.experimental.pallas import tpu as pltpu

HEAD_DIM = 128
BLK = 128
A_KV_HEADS = 2
A_WINDOW = 128
N_BRANCH = 3
CMP_LEN = 32
CMP_STRIDE = 16
CMP_HIDDEN = 256
SEL_BLOCK = 64
SEL_TOPK = 16
B_WINDOW = 512
C_PATTERNS = ((128, 1), (512, 4), (2048, 16))
N_EXPERTS = 16
N_GROUPS = 4
EXPERTS_PER_GROUP = N_EXPERTS // N_GROUPS
TOP_K = 2
LN_EPS = 1e-5
NEG = -1e30
M_INIT = -1e20
LOG2E = math.log2(math.e)
Q_SCALE = HEAD_DIM ** -0.5 * LOG2E

LANES = 128
VMEM_BUDGET = 56 * 1024 * 1024

F32 = jnp.float32
BF16 = jnp.bfloat16
NT_DIMS = (((1,), (1,)), ((), ()))


def _params(n_grid, vmem_bytes):
    return pltpu.CompilerParams(
        dimension_semantics=("arbitrary",) * n_grid,
        vmem_limit_bytes=int(min(VMEM_BUDGET, max(vmem_bytes, 16 * 1024 * 1024))),
    )


def _alibi_slopes2(n):
    return jnp.asarray(2.0 ** (-8.0 * np.arange(1, n + 1) / n) * LOG2E, dtype=F32)


def _layer_norm_rows(z, g, b):
    mu = jnp.mean(z, axis=-1, keepdims=True)
    zc = z - mu
    var = jnp.mean(zc * zc, axis=-1, keepdims=True)
    return zc * lax.rsqrt(var + LN_EPS) * g + b


HI_HALF = 0xFFFF0000


def _pack_bf16_pairs(x):
    n = x.shape[1] // 2
    lo = lax.bitcast_convert_type(x[:, :n].astype(BF16).astype(F32), jnp.uint32) >> 16
    hi = lax.bitcast_convert_type(x[:, n:].astype(BF16).astype(F32), jnp.uint32) & jnp.uint32(HI_HALF)
    return lo | hi


def _unpack_bf16_pairs(pk):
    lo = lax.bitcast_convert_type(pk << 16, F32)
    hi = lax.bitcast_convert_type(pk & jnp.uint32(HI_HALF), F32)
    return jnp.concatenate([lo, hi], axis=1)


def _lane_tile(x, n):
    return x if n == LANES else jnp.concatenate([x] * (n // LANES), axis=1)


def _cast_kernel(w_ref, o_ref, *, chunks, n_used):
    dst = 0
    for src, size, scale in chunks:
        blk = w_ref[0, :, src:src + size]
        if scale != 1.0:
            blk = blk * scale
        o_ref[:, dst:dst + size] = blk.astype(o_ref.dtype)
        dst += size
    assert dst == n_used
    if n_used < o_ref.shape[1]:
        o_ref[:, n_used:] = jnp.zeros((o_ref.shape[0], o_ref.shape[1] - n_used), o_ref.dtype)


def cast_weight(w_stacked, layer, chunks=None, pad_to=1):
    _, k, n = w_stacked.shape
    if chunks is None:
        chunks = ((0, n, 1.0),)
    n_used = sum(c[1] for c in chunks)
    n_out = -(-n_used // pad_to) * pad_to
    tr = 128 if n > 8192 else 256
    tr = min(tr, k)
    kern = functools.partial(_cast_kernel, chunks=tuple(chunks), n_used=n_used)
    vmem = 2 * (tr * n * 4 + tr * n_out * 2) + tr * n * 4
    return pl.pallas_call(
        kern,
        grid=(k // tr,),
        in_specs=[pl.BlockSpec((1, tr, n), lambda r: (layer, r, 0))],
        out_specs=pl.BlockSpec((tr, n_out), lambda r: (r, 0)),
        out_shape=jax.ShapeDtypeStruct((k, n_out), BF16),
        compiler_params=_params(1, vmem + (8 << 20)),
        name="cast_weight",
    )(w_stacked)


def _proj_kernel(x_ref, w_ref, o_ref, *, n_q_tiles):
    w = w_ref[...] if len(w_ref.shape) == 2 else w_ref[0].astype(BF16)
    acc = jnp.dot(x_ref[...], w, preferred_element_type=F32)
    if n_q_tiles:
        acc = acc * jnp.where(pl.program_id(1) < n_q_tiles, Q_SCALE, 1.0)
    for c in range(o_ref.shape[0]):
        o_ref[c] = acc[:, c * LANES:(c + 1) * LANES].astype(o_ref.dtype)


def proj_matmul(x, w, out_dtype, layer=None, n_q_cols=0):
    m, k = x.shape
    n = w.shape[-1]
    tm = min(1024, m)
    tn = min(512, n)
    assert n_q_cols % tn == 0
    osz = jnp.dtype(out_dtype).itemsize
    if layer is None:
        w_spec = pl.BlockSpec((k, tn), lambda i, j: (0, j))
        vmem = 2 * (tm * k * 2 + k * tn * 2 + tm * tn * osz) + 2 * tm * tn * 4
    else:
        w_spec = pl.BlockSpec((1, k, tn), lambda i, j: (layer, 0, j))
        vmem = 2 * (tm * k * 2 + k * tn * 4 + tm * tn * osz) + k * tn * 2 + 2 * tm * tn * 4
    return pl.pallas_call(
        functools.partial(_proj_kernel, n_q_tiles=n_q_cols // tn),
        grid=(m // tm, n // tn),
        in_specs=[pl.BlockSpec((tm, k), lambda i, j: (i, 0)), w_spec],
        out_specs=pl.BlockSpec((tn // LANES, tm, LANES), lambda i, j: (j, i, 0)),
        out_shape=jax.ShapeDtypeStruct((n // LANES, m, LANES), out_dtype),
        compiler_params=_params(2, vmem + (6 << 20)),
        name="proj_matmul",
    )(x, w)


def _banded_kernel(slopes_ref, sinks_ref, q_ref, kc_ref, kp_ref, vc_ref, vp_ref, *rest,
                   G, R, PR, window, has_sink, gate_col):
    if gate_col is not None:
        g_ref, oacc_ref, o_ref, kbuf, vbuf = rest
    else:
        o_ref, kbuf, vbuf = rest
    kv = pl.program_id(0)
    i = pl.program_id(1)
    kn = PR + BLK
    kbuf[0:PR, :] = kp_ref[0]
    kbuf[PR:PR + R, :] = kc_ref[0]
    vbuf[0:PR, :] = vp_ref[0]
    vbuf[PR:PR + R, :] = vc_ref[0]
    qi = lax.broadcasted_iota(jnp.int32, (BLK, kn), 0)
    kc = lax.broadcasted_iota(jnp.int32, (BLK, kn), 1)
    dist = PR + qi - kc
    static_valid = (dist >= 0) & (dist < window)
    negd = -dist.astype(F32)

    def unit(j, carry):
        row0 = pl.multiple_of(j * BLK, BLK)
        first_key = PR - (i * R + j * BLK)
        valid = static_valid & (kc >= first_key)
        k = kbuf[pl.ds(row0, kn), :]
        v = vbuf[pl.ds(row0, kn), :]
        q = q_ref[:, pl.ds(row0, BLK), :].reshape(G * BLK, HEAD_DIM)
        s_all = lax.dot_general(q, k, NT_DIMS, preferred_element_type=F32)
        ps, ls = [], []
        for g in range(G):
            h = kv * G + g
            s = s_all[g * BLK:(g + 1) * BLK] + slopes_ref[h] * negd
            s = jnp.where(valid, s, NEG)
            m = jnp.max(s, axis=-1, keepdims=True)
            p = jnp.exp2(s - m)
            l = jnp.sum(p, axis=-1, keepdims=True)
            if has_sink:
                l = l + jnp.exp2(sinks_ref[h] - m)
            ps.append(p.astype(BF16))
            ls.append(l)
        acc = jnp.dot(jnp.concatenate(ps, axis=0), v, preferred_element_type=F32)
        for g in range(G):
            o = acc[g * BLK:(g + 1) * BLK] / ls[g]
            if gate_col is not None:
                gl = g_ref[0, pl.ds(row0, BLK), :].astype(F32)
                c = g * N_BRANCH + gate_col
                o = oacc_ref[g, pl.ds(row0, BLK), :] + jax.nn.sigmoid(gl[:, c:c + 1]) * o
            o_ref[g, pl.ds(row0, BLK), :] = o.astype(o_ref.dtype)
        return carry

    lax.fori_loop(0, R // BLK, unit, 0)


def banded_attention(proj, slopes2, sinks2, *, q_base, k_blk, v_blk, n_kv, G, window,
                     has_sink=False, gate_blk=None, gate_col=None, oacc=None):
    L = proj.shape[1]
    n_prev = -(-(window - 1) // BLK)
    PR = n_prev * BLK
    R = max(PR, min(L, 8192 // G))
    assert L % R == 0 and R % PR == 0 and q_base % G == 0
    qb0 = q_base // G
    rp = R // PR
    in_specs = [
        pl.BlockSpec((G, R, LANES), lambda kv, i, *_: (qb0 + kv, i, 0)),
        pl.BlockSpec((1, R, LANES), lambda kv, i, *_: (k_blk + kv, i, 0)),
        pl.BlockSpec((1, PR, LANES), lambda kv, i, *_: (k_blk + kv, jnp.maximum(i * rp - 1, 0), 0)),
        pl.BlockSpec((1, R, LANES), lambda kv, i, *_: (v_blk + kv, i, 0)),
        pl.BlockSpec((1, PR, LANES), lambda kv, i, *_: (v_blk + kv, jnp.maximum(i * rp - 1, 0), 0)),
    ]
    args = [proj, proj, proj, proj, proj]
    vmem = 2 * (G * R * LANES * 2 * 2 + 4 * (R + PR) * LANES * 2) + 2 * (R + PR) * LANES * 2
    if gate_col is not None:
        assert n_kv == 1
        in_specs += [pl.BlockSpec((1, R, LANES), lambda kv, i, *_: (gate_blk, i, 0)),
                     pl.BlockSpec((G, R, LANES), lambda kv, i, *_: (kv, i, 0))]
        args += [proj, oacc]
        vmem += 2 * (R * LANES * 2 + G * R * LANES * 4)
    kern = functools.partial(_banded_kernel, G=G, R=R, PR=PR, window=window,
                             has_sink=has_sink, gate_col=gate_col)
    return pl.pallas_call(
        kern,
        grid_spec=pltpu.PrefetchScalarGridSpec(
            num_scalar_prefetch=2,
            grid=(n_kv, L // R),
            in_specs=in_specs,
            out_specs=pl.BlockSpec((G, R, LANES), lambda kv, i, *_: (kv, i, 0)),
            scratch_shapes=[pltpu.VMEM((PR + R, LANES), BF16), pltpu.VMEM((PR + R, LANES), BF16)],
        ),
        out_shape=jax.ShapeDtypeStruct((n_kv * G, L, LANES), BF16),
        compiler_params=_params(2, vmem + (16 << 20)),
        name="banded_attention",
    )(slopes2, sinks2, *args)


def _compress_kernel(g_ref, pe_ref, w1_ref, b1_ref, w2_ref, b2_ref, o_ref):
    ng = g_ref.shape[1]
    half = w1_ref.shape[1] // 2
    g = g_ref[0]
    y0 = jnp.dot(g, w1_ref[0, 0:half, :], preferred_element_type=F32)
    y1 = jnp.dot(g, w1_ref[0, half:, :], preferred_element_type=F32)
    y1 = pltpu.roll(y1, ng - 1, 0)
    pe = jnp.broadcast_to(pe_ref[0], (8, pe_ref.shape[2]))
    ype = jnp.dot(pe, w1_ref[0], preferred_element_type=F32)[0:1]
    h = jax.nn.gelu(y0 + y1 + ype + b1_ref[0])
    out = jnp.dot(h.astype(BF16), w2_ref[0], preferred_element_type=F32) + b2_ref[0]
    o_ref[0] = out.astype(o_ref.dtype)


def nsa_compress(groups, pe, w1, b1, w2, b2):
    _, ng, gw = groups.shape
    hid = w1.shape[2]
    vmem = 2 * (ng * gw * 2 + 2 * gw * hid * 2 + 2 * gw * 2) + 6 * ng * hid * 4
    return pl.pallas_call(
        _compress_kernel,
        grid=(2,),
        in_specs=[pl.BlockSpec((1, ng, gw), lambda c: (c, 0, 0)),
                  pl.BlockSpec((1, 1, 2 * gw), lambda c: (c, 0, 0)),
                  pl.BlockSpec((1, 2 * gw, hid), lambda c: (c, 0, 0)),
                  pl.BlockSpec((1, 1, hid), lambda c: (c, 0, 0)),
                  pl.BlockSpec((1, hid, HEAD_DIM), lambda c: (c, 0, 0)),
                  pl.BlockSpec((1, 1, HEAD_DIM), lambda c: (c, 0, 0))],
        out_specs=pl.BlockSpec((1, ng, HEAD_DIM), lambda c: (c, 0, 0)),
        out_shape=jax.ShapeDtypeStruct((2, ng, HEAD_DIM), BF16),
        compiler_params=_params(1, vmem + (8 << 20)),
        name="nsa_compress",
    )(groups, pe, w1, b1, w2, b2)


def _cmp_kernel(slopes_ref, q_ref, kv_ref, g_ref, ov_ref, o_ref, sel_ref, *, H, TQ, n_cmp, n_sel):
    qb = pl.program_id(0)
    ncp = kv_ref.shape[1]
    nsp = sel_ref.shape[1]
    kc = kv_ref[0]
    vc = kv_ref[1]
    t = qb * TQ + lax.broadcasted_iota(jnp.int32, (TQ, 1), 0)
    n = lax.broadcasted_iota(jnp.int32, (1, ncp), 1)
    cdist = t - (n * CMP_STRIDE + (CMP_LEN - 1))
    valid = (cdist >= 0) & (n < n_cmp)
    negd = -jnp.maximum(cdist, 0).astype(F32)
    gl = g_ref[0].astype(F32)
    psum = jnp.zeros((TQ, ncp), F32)
    for h in range(H):
        s = lax.dot_general(q_ref[h], kc, NT_DIMS, preferred_element_type=F32)
        s = jnp.where(valid, s + slopes_ref[h] * negd, NEG)
        m = jnp.max(s, axis=-1, keepdims=True)
        e = jnp.exp2(s - m)
        l = jnp.sum(e, axis=-1, keepdims=True)
        p = jnp.where(valid, e / l, 0.0)
        psum = psum + p
        o = jnp.dot(p.astype(BF16), vc, preferred_element_type=F32)
        c = h * N_BRANCH
        o_ref[h] = jax.nn.sigmoid(gl[:, c:c + 1]) * o
    p_hi = psum.astype(BF16)
    p_lo = (psum - p_hi.astype(F32)).astype(BF16)
    imp = (jnp.dot(p_hi, ov_ref[...], preferred_element_type=F32)
           + jnp.dot(p_lo, ov_ref[...], preferred_element_type=F32))
    j = lax.broadcasted_iota(jnp.int32, (1, nsp), 1)
    jf = j.astype(F32)
    cur = t // SEL_BLOCK
    forced = (j == 0) | (j == cur) | (j == cur - 1)
    svalid = j * SEL_BLOCK <= t
    work = jnp.where(forced, 1e30, jnp.where(svalid, imp, -1.0))
    work = jnp.where(j < n_sel, work, -2.0)
    sel = jnp.zeros((TQ, nsp), F32)
    for _ in range(min(SEL_TOPK, n_sel)):
        mx = jnp.max(work, axis=-1, keepdims=True)
        idx = jnp.min(jnp.where(work == mx, jf, 1e9), axis=-1, keepdims=True)
        hit = jf == idx
        sel = jnp.where(hit, 1.0, sel)
        work = jnp.where(hit, -2.0, work)
    sel_ref[...] = sel.astype(sel_ref.dtype)


def nsa_cmp_attention(proj, kvc, slopes2, overlap, *, q_base, H, gate_blk, n_cmp, n_sel):
    L = proj.shape[1]
    ncp = kvc.shape[1]
    nsp = overlap.shape[1]
    TQ = min(256, L)
    kern = functools.partial(_cmp_kernel, H=H, TQ=TQ, n_cmp=n_cmp, n_sel=n_sel)
    vmem = 2 * (H * TQ * LANES * 2 + 2 * ncp * LANES * 2 + TQ * LANES * 2 + ncp * nsp * 2
                + H * TQ * LANES * 4 + TQ * nsp * 2) + 8 * TQ * ncp * 4
    return pl.pallas_call(
        kern,
        grid_spec=pltpu.PrefetchScalarGridSpec(
            num_scalar_prefetch=1,
            grid=(L // TQ,),
            in_specs=[pl.BlockSpec((H, TQ, LANES), lambda qb, *_: (q_base // H, qb, 0)),
                      pl.BlockSpec((2, ncp, LANES), lambda qb, *_: (0, 0, 0)),
                      pl.BlockSpec((1, TQ, LANES), lambda qb, *_: (gate_blk, qb, 0)),
                      pl.BlockSpec((ncp, nsp), lambda qb, *_: (0, 0))],
            out_specs=[pl.BlockSpec((H, TQ, LANES), lambda qb, *_: (0, qb, 0)),
                       pl.BlockSpec((TQ, nsp), lambda qb, *_: (qb, 0))],
        ),
        out_shape=[jax.ShapeDtypeStruct((H, L, LANES), F32),
                   jax.ShapeDtypeStruct((L, nsp), BF16)],
        compiler_params=_params(1, vmem + (8 << 20)),
        name="nsa_cmp_attention",
    )(slopes2, proj, kvc, proj, overlap)


def _sel_kernel(q_ref, qx_ref, k_ref, v_ref, unsel_ref, g_ref, oacc_ref, o_ref, qs, m_s, acc_s, *, H, TQ, TK):
    qb = pl.program_id(0)
    kb = pl.program_id(1)
    nkb = pl.num_programs(1)

    @pl.when(kb == 0)
    def _():
        for h in range(H):
            qs[h * TQ:(h + 1) * TQ, 0:LANES] = q_ref[h]
        m_s[...] = jnp.full(m_s.shape, M_INIT, F32)
        acc_s[...] = jnp.zeros(acc_s.shape, F32)

    def step(causal):
        for h in range(H):
            qs[h * TQ:(h + 1) * TQ, LANES:2 * LANES] = qx_ref[h] + unsel_ref[0]
        s_all = lax.dot_general(qs[...], k_ref[...], NT_DIMS, preferred_element_type=F32)
        if causal:
            t = qb * TQ + lax.broadcasted_iota(jnp.int32, (TQ, 1), 0)
            kk = kb * TK + lax.broadcasted_iota(jnp.int32, (1, TK), 1)
            cmask = kk <= t
        ps = []
        for h in range(H):
            rows = slice(h * TQ, (h + 1) * TQ)
            s = s_all[rows]
            if causal:
                s = jnp.where(cmask, s, NEG)
            m_old = m_s[rows, :]
            m_new = jnp.maximum(m_old, jnp.max(s, axis=-1, keepdims=True))
            m_s[rows, :] = m_new
            alpha = jnp.exp2(m_old - m_new)
            acc_s[rows, :] = acc_s[rows, :] * _lane_tile(alpha, 2 * LANES)
            ps.append(jnp.exp2(s - _lane_tile(m_new, TK)).astype(BF16))
        acc_s[...] += jnp.dot(jnp.concatenate(ps, axis=0), v_ref[...], preferred_element_type=F32)

    last_key = kb * TK + (TK - 1)

    @pl.when(last_key <= qb * TQ)
    def _():
        step(False)

    @pl.when((last_key > qb * TQ) & (kb * TK <= qb * TQ + (TQ - 1)))
    def _():
        step(True)

    @pl.when(kb == nkb - 1)
    def _():
        gl = g_ref[0].astype(F32)
        for h in range(H):
            rows = slice(h * TQ, (h + 1) * TQ)
            c = h * N_BRANCH + 1
            o = acc_s[rows, 0:LANES] / acc_s[rows, LANES:2 * LANES]
            o_ref[h] = oacc_ref[h] + jax.nn.sigmoid(gl[:, c:c + 1]) * o


SEL_TQ = 128
SEL_TK = 1024
N_POS_COLS = 6
UNSEL_LOGIT = -(2.0 ** 100)


def nsa_sel_attention(proj, kx, vx, qx, unsel, oacc, *, q_base, H, gate_blk):
    L = proj.shape[1]
    TQ = min(SEL_TQ, L)
    TK = min(SEL_TK, L)

    def last_kb(qb):
        return (qb * TQ + TQ - 1) // TK

    kern = functools.partial(_sel_kernel, H=H, TQ=TQ, TK=TK)
    vmem = (2 * (H * TQ * LANES * 2 + 2 * TK * 2 * LANES * 2 + 2 * TQ * LANES * 2
                 + 2 * H * TQ * LANES * 4) + H * TQ * LANES * (4 + 4 + 8)
            + H * TQ * TK * (4 + 4 + 2))
    return pl.pallas_call(
        kern,
        grid=(L // TQ, L // TK),
        in_specs=[pl.BlockSpec((H, TQ, LANES), lambda qb, kb: (q_base // H, qb, 0)),
                  pl.BlockSpec((H, 1, LANES), lambda qb, kb: (0, 0, 0)),
                  pl.BlockSpec((TK, 2 * LANES), lambda qb, kb: (jnp.minimum(kb, last_kb(qb)), 0)),
                  pl.BlockSpec((TK, 2 * LANES), lambda qb, kb: (jnp.minimum(kb, last_kb(qb)), 0)),
                  pl.BlockSpec((1, TQ, LANES), lambda qb, kb: (jnp.minimum(kb, last_kb(qb)), qb, 0)),
                  pl.BlockSpec((1, TQ, LANES), lambda qb, kb: (gate_blk, qb, 0)),
                  pl.BlockSpec((H, TQ, LANES), lambda qb, kb: (0, qb, 0))],
        out_specs=pl.BlockSpec((H, TQ, LANES), lambda qb, kb: (0, qb, 0)),
        scratch_shapes=[pltpu.VMEM((H * TQ, 2 * LANES), BF16), pltpu.VMEM((H * TQ, LANES), F32),
                        pltpu.VMEM((H * TQ, 2 * LANES), F32)],
        out_shape=jax.ShapeDtypeStruct((H, L, LANES), F32),
        compiler_params=_params(2, vmem + (4 << 20)),
        name="nsa_sel_attention",
    )(proj, qx, kx, vx, unsel, proj, oacc)


def _sel_extra_columns(L, slopes2, sel, n_sel):
    TK = min(SEL_TK, L)
    nj = TK // SEL_BLOCK
    pos = np.arange(L)
    kcols = np.zeros((L, LANES), np.float32)
    kcols[:, 0:3] = ((pos // LANES) * LANES)[:, None]
    kcols[:, 3:6] = (pos % LANES)[:, None]
    kcols[pos, N_POS_COLS + (pos // SEL_BLOCK) % nj] = UNSEL_LOGIT
    a1 = slopes2.astype(BF16)
    r1 = slopes2 - a1.astype(F32)
    a2 = r1.astype(BF16)
    a3 = (r1 - a2.astype(F32)).astype(BF16)
    parts = jnp.stack([a1, a2, a3, a1, a2, a3], axis=1)
    qx = jnp.zeros((slopes2.shape[0], LANES), BF16).at[:, 0:N_POS_COLS].set(parts)
    flags = (1.0 - sel[:, :n_sel].astype(F32)).reshape(L, L // TK, nj).transpose(1, 0, 2)
    unsel = jnp.pad(flags, ((0, 0), (0, 0), (N_POS_COLS, LANES - N_POS_COLS - nj))).astype(BF16)
    return jnp.asarray(kcols, dtype=BF16), qx.reshape(-1, 1, LANES), unsel


def _dilated_kernel(slopes_ref, q_ref, k_ref, v_ref, o_ref, q4, k4, v4, ktail, vtail, o_s, l_s, *, CH):
    h = pl.program_id(0)
    i = pl.program_id(1)
    par = i % 2
    slope = slopes_ref[h]
    sub = CH // 4

    @pl.when(i == 0)
    def _():
        k4[1] = jnp.zeros(k4.shape[1:], F32)
        v4[1] = jnp.zeros(v4.shape[1:], F32)
        ktail[...] = jnp.zeros(ktail.shape, F32)
        vtail[...] = jnp.zeros(vtail.shape, F32)

    for r in range(4):
        q4[r] = q_ref[0, pl.ds(r, sub, stride=4), :]
        k4[par, r] = k_ref[0, pl.ds(r, sub, stride=4), :]
        v4[par, r] = v_ref[0, pl.ds(r, sub, stride=4), :]

    kn = 2 * BLK
    qi = lax.broadcasted_iota(jnp.int32, (BLK, kn), 0)
    kc = lax.broadcasted_iota(jnp.int32, (BLK, kn), 1)
    dist = BLK + qi - kc
    static_valid = (dist >= 0) & (dist <= BLK)
    first_valid = static_valid & (kc >= jnp.where(i == 0, BLK, 0))
    negd = -dist.astype(F32)
    ones = jnp.ones((kn, LANES), BF16)

    def unit(p, bias, q, kprev, kcur, vprev, vcur, rows):
        kk = jnp.concatenate([kprev.astype(BF16), kcur.astype(BF16)], axis=0)
        vv = jnp.concatenate([vprev.astype(BF16), vcur.astype(BF16)], axis=0)
        s = lax.dot_general(q.astype(BF16), kk, NT_DIMS, preferred_element_type=F32) + bias
        m = jnp.max(s, axis=-1, keepdims=True)
        e = jnp.exp2(s - m).astype(BF16)
        acc = jnp.dot(e, jnp.concatenate([vv, ones], axis=1), preferred_element_type=F32)
        l = acc[:, LANES:]
        o_s[p, rows, :] = acc[:, :LANES] / l
        l_s[p, rows, :] = m + jnp.log2(l)

    for p, (window, dil) in enumerate(C_PATTERNS):
        assert window // dil == BLK
        ab = (slope * dil) * negd
        bias = jnp.where(static_valid, ab, NEG)
        bias0 = jnp.where(first_valid, ab, NEG)
        if dil == 1:
            for j in range(CH // BLK):
                cur = slice(j * BLK, (j + 1) * BLK)
                prev = slice((j - 1) * BLK, j * BLK)
                unit(p, bias if j else bias0, q_ref[0, cur, :],
                     k_ref[0, prev, :] if j else ktail[...], k_ref[0, cur, :],
                     v_ref[0, prev, :] if j else vtail[...], v_ref[0, cur, :], pl.ds(j * BLK, BLK))
        elif dil == 4:
            for r in range(4):
                for j in range(sub // BLK):
                    cur = slice(j * BLK, (j + 1) * BLK)
                    prev = slice((j - 1) * BLK, j * BLK)
                    last = slice(sub - BLK, sub)
                    unit(p, bias if j else bias0, q4[r, cur, :],
                         k4[par, r, prev, :] if j else k4[1 - par, r, last, :], k4[par, r, cur, :],
                         v4[par, r, prev, :] if j else v4[1 - par, r, last, :], v4[par, r, cur, :],
                         pl.ds(r + 4 * j * BLK, BLK, stride=4))
        else:
            assert dil == 16 and sub == 4 * BLK
            for r in range(16):
                r4, a = r % 4, r // 4
                rows4 = pl.ds(a, BLK, stride=4)
                unit(p, bias0, q4[r4, rows4, :],
                     k4[1 - par, r4, rows4, :], k4[par, r4, rows4, :],
                     v4[1 - par, r4, rows4, :], v4[par, r4, rows4, :], pl.ds(r, BLK, stride=16))

    ktail[...] = k_ref[0, CH - BLK:CH, :]
    vtail[...] = v_ref[0, CH - BLK:CH, :]
    lses = [l_s[p] for p in range(len(C_PATTERNS))]
    mx = functools.reduce(jnp.maximum, lses)
    ws = [jnp.exp2(x - mx) for x in lses]
    num = functools.reduce(lambda a, b: a + b, [w * o_s[p] for p, w in enumerate(ws)])
    den = functools.reduce(lambda a, b: a + b, ws)
    o_ref[0] = (num / den).astype(o_ref.dtype)


def dilated_attention(qkv, slopes2, *, H):
    L = qkv.shape[1]
    CH = max(d for _, d in C_PATTERNS) * BLK
    assert L % CH == 0
    kern = functools.partial(_dilated_kernel, CH=CH)
    npat = len(C_PATTERNS)
    vmem = 2 * (3 * CH * LANES * 4 + CH * LANES * 2) + (5 + 2 * npat) * CH * LANES * 4
    return pl.pallas_call(
        kern,
        grid_spec=pltpu.PrefetchScalarGridSpec(
            num_scalar_prefetch=1,
            grid=(H, L // CH),
            in_specs=[pl.BlockSpec((1, CH, LANES), lambda h, i, *_: (h, i, 0)),
                      pl.BlockSpec((1, CH, LANES), lambda h, i, *_: (H + h, i, 0)),
                      pl.BlockSpec((1, CH, LANES), lambda h, i, *_: (2 * H + h, i, 0))],
            out_specs=pl.BlockSpec((1, CH, LANES), lambda h, i, *_: (h, i, 0)),
            scratch_shapes=[pltpu.VMEM((4, CH // 4, LANES), F32),
                            pltpu.VMEM((2, 4, CH // 4, LANES), F32),
                            pltpu.VMEM((2, 4, CH // 4, LANES), F32),
                            pltpu.VMEM((BLK, LANES), F32), pltpu.VMEM((BLK, LANES), F32),
                            pltpu.VMEM((npat, CH, LANES), F32), pltpu.VMEM((npat, CH, LANES), F32)],
        ),
        out_shape=jax.ShapeDtypeStruct((H, L, LANES), BF16),
        compiler_params=_params(2, vmem + (16 << 20)),
        name="dilated_attention",
    )(slopes2, qkv, qkv, qkv)


def _outproj_kernel(*refs):
    *o_refs, w_ref, y_ref = refs
    lhs = jnp.concatenate([o_ref[c] for o_ref in o_refs for c in range(o_ref.shape[0])], axis=1)
    y_ref[...] = jnp.dot(lhs, w_ref[...], preferred_element_type=F32)


def outproj(o_parts, w):
    m = o_parts[0].shape[1]
    k, n = w.shape
    assert k == sum(o.shape[0] for o in o_parts) * LANES
    tm = min(1024, m)
    tn = min(512, n)
    vmem = 2 * (tm * k * 2 + k * tn * 2 + tm * tn * 4) + tm * k * 2 + tm * tn * 4
    return pl.pallas_call(
        _outproj_kernel,
        grid=(m // tm, n // tn),
        in_specs=[pl.BlockSpec((o.shape[0], tm, LANES), lambda i, j: (0, i, 0)) for o in o_parts]
        + [pl.BlockSpec((k, tn), lambda i, j: (0, j))],
        out_specs=pl.BlockSpec((tm, tn), lambda i, j: (i, j)),
        out_shape=jax.ShapeDtypeStruct((m, n), F32),
        compiler_params=_params(2, vmem + (8 << 20)),
        name="outproj",
    )(*o_parts, w)


def _route(x, wt, b):
    logits = lax.dot_general(wt, x, NT_DIMS, preferred_element_type=F32,
                             precision=lax.Precision.HIGHEST) + b
    mx = jnp.max(logits, axis=0, keepdims=True)
    e = jnp.exp(logits - mx)
    probs = e / jnp.sum(e, axis=0, keepdims=True)
    rows = [probs[i:i + 1, :] for i in range(N_EXPERTS)]
    epg = EXPERTS_PER_GROUP
    best = None
    gsel = None
    for gi in range(N_GROUPS):
        grp = rows[gi * epg:(gi + 1) * epg]
        score = None
        for a in range(epg):
            for c in range(a + 1, epg):
                pair = grp[a] + grp[c]
                score = pair if score is None else jnp.maximum(score, pair)
        if best is None:
            best, gsel = score, jnp.zeros(score.shape, jnp.int32)
        else:
            better = score > best
            gsel = jnp.where(better, gi, gsel)
            best = jnp.maximum(best, score)
    ing = []
    for kk in range(epg):
        val = rows[kk]
        for gi in range(1, N_GROUPS):
            val = jnp.where(gsel == gi, rows[gi * epg + kk], val)
        ing.append(val)

    def argmax_first(vals):
        bv, bi = vals[0], jnp.zeros(vals[0].shape, jnp.int32)
        for kk in range(1, len(vals)):
            better = vals[kk] > bv
            bi = jnp.where(better, kk, bi)
            bv = jnp.maximum(bv, vals[kk])
        return bv, bi

    v1, i1 = argmax_first(ing)
    rest = [jnp.where(i1 == kk, -1.0, ing[kk]) for kk in range(epg)]
    v2, i2 = argmax_first(rest)
    tot = v1 + v2
    return (gsel * epg + i1, gsel * epg + i2), (v1 / tot, v2 / tot)


def _ln_router_kernel(y_ref, x_ref, g_ref, b_ref, wt_ref, rb_ref, xo_ref, xp_ref, eid_ref, gate_ref, *, alpha):
    xn = _layer_norm_rows(alpha * x_ref[...] + y_ref[...], g_ref[...], b_ref[...])
    xo_ref[...] = xn
    xp_ref[...] = _pack_bf16_pairs(xn)
    eids, gates = _route(xn, wt_ref[...], rb_ref[...])
    for kk in range(TOP_K):
        eid_ref[kk:kk + 1, :] = eids[kk]
        gate_ref[kk:kk + 1, :] = gates[kk]


def ln_router(y, x, g, b, wt, rb, alpha):
    m, d = x.shape
    tr = min(256, m)
    kern = functools.partial(_ln_router_kernel, alpha=alpha)
    vmem = 2 * (3 * tr * d * 4 + tr * d * 2 + N_EXPERTS * d * 4) + 8 * tr * d * 2 + 3 * tr * d * 4
    return pl.pallas_call(
        kern,
        grid=(m // tr,),
        in_specs=[pl.BlockSpec((tr, d), lambda i: (i, 0)),
                  pl.BlockSpec((tr, d), lambda i: (i, 0)),
                  pl.BlockSpec((1, d), lambda i: (0, 0)),
                  pl.BlockSpec((1, d), lambda i: (0, 0)),
                  pl.BlockSpec((N_EXPERTS, d), lambda i: (0, 0)),
                  pl.BlockSpec((N_EXPERTS, 1), lambda i: (0, 0))],
        out_specs=[pl.BlockSpec((tr, d), lambda i: (i, 0)),
                   pl.BlockSpec((tr, d // 2), lambda i: (i, 0)),
                   pl.BlockSpec((TOP_K, tr), lambda i: (0, i)),
                   pl.BlockSpec((TOP_K, tr), lambda i: (0, i))],
        out_shape=[jax.ShapeDtypeStruct((m, d), F32), jax.ShapeDtypeStruct((m, d // 2), jnp.uint32),
                   jax.ShapeDtypeStruct((TOP_K, m), jnp.int32), jax.ShapeDtypeStruct((TOP_K, m), F32)],
        compiler_params=_params(1, vmem + (4 << 20)),
        name="ln_router",
    )(y, x, g, b, wt, rb)


def _row_copy(src_hbm, row, dst, slot, r, sem):
    return pltpu.make_async_copy(src_hbm.at[pl.ds(row, 1)], dst.at[slot, pl.ds(r, 1)], sem.at[slot])


def _gather_start(src_hbm, idx_ref, dst, slot, n_rows, sem):
    def body(r, c):
        _row_copy(src_hbm, idx_ref[0, 0, r], dst, slot, r, sem).start()
        return c
    lax.fori_loop(0, n_rows, body, 0, unroll=8)


def _gather_wait(src_hbm, dst, slot, n_rows, sem):
    pltpu.make_async_copy(src_hbm.at[pl.ds(0, n_rows)], dst.at[slot], sem.at[slot]).wait()


def _moe_gate_kernel(te_ref, nt_ref, idx_ref, idx_next_ref, x_hbm, wg_ref, g_ref, xs_ref, xbuf, sem, *, TM):
    t = pl.program_id(0)
    nt = nt_ref[0]
    slot = t % 2

    @pl.when((t == 0) & (nt > 0))
    def _():
        _gather_start(x_hbm, idx_ref, xbuf, 0, TM, sem)

    @pl.when(t + 1 < nt)
    def _():
        _gather_start(x_hbm, idx_next_ref, xbuf, 1 - slot, TM, sem)

    @pl.when(t < nt)
    def _():
        _gather_wait(x_hbm, xbuf, slot, TM, sem)
        xb = _unpack_bf16_pairs(xbuf[slot]).astype(BF16)
        xs_ref[...] = xb
        g_ref[...] = jax.nn.silu(jnp.dot(xb, wg_ref[0].astype(BF16), preferred_element_type=F32))

    @pl.when(t >= nt)
    def _():
        xs_ref[...] = jnp.zeros(xs_ref.shape, xs_ref.dtype)
        g_ref[...] = jnp.zeros(g_ref.shape, g_ref.dtype)


def moe_gate(xp, w_gate, e0, tile_expert, n_tiles, row_token, TM):
    dp = xp.shape[1]
    d = 2 * dp
    f = w_gate.shape[2]
    tmax = tile_expert.shape[0]
    kern = functools.partial(_moe_gate_kernel, TM=TM)
    vmem = 2 * (d * f * 4 + TM * f * 4 + TM * d * 2) + d * f * 2 + 2 * TM * dp * 4 + 3 * TM * d * 4 + 2 * TM * f * 4
    return pl.pallas_call(
        kern,
        grid_spec=pltpu.PrefetchScalarGridSpec(
            num_scalar_prefetch=2,
            grid=(tmax,),
            in_specs=[pl.BlockSpec((1, 1, TM), lambda t, te, nt: (t, 0, 0), memory_space=pltpu.SMEM),
                      pl.BlockSpec((1, 1, TM), lambda t, te, nt: (jnp.minimum(t + 1, tmax - 1), 0, 0),
                                   memory_space=pltpu.SMEM),
                      pl.BlockSpec(memory_space=pl.ANY),
                      pl.BlockSpec((1, d, f), lambda t, te, nt: (e0 + te[t], 0, 0))],
            out_specs=[pl.BlockSpec((TM, f), lambda t, te, nt: (t, 0)),
                       pl.BlockSpec((TM, d), lambda t, te, nt: (t, 0))],
            scratch_shapes=[pltpu.VMEM((2, TM, dp), jnp.uint32), pltpu.SemaphoreType.DMA((2,))],
        ),
        out_shape=[jax.ShapeDtypeStruct((tmax * TM, f), F32), jax.ShapeDtypeStruct((tmax * TM, d), BF16)],
        compiler_params=_params(1, vmem + (6 << 20)),
        name="moe_gate",
    )(tile_expert, n_tiles, row_token, row_token, xp, w_gate)


def _moe_up_kernel(te_ref, nt_ref, xs_ref, g_ref, wu_ref, act_ref):
    t = pl.program_id(0)

    @pl.when(t < nt_ref[0])
    def _():
        hu = jnp.dot(xs_ref[...], wu_ref[0].astype(BF16), preferred_element_type=F32)
        act_ref[...] = (g_ref[...] * hu).astype(act_ref.dtype)

    @pl.when(t >= nt_ref[0])
    def _():
        act_ref[...] = jnp.zeros(act_ref.shape, act_ref.dtype)


def moe_up(xs, g, w_up, e0, tile_expert, n_tiles, TM):
    d = xs.shape[1]
    f = w_up.shape[2]
    tmax = tile_expert.shape[0]
    vmem = 2 * (d * f * 4 + TM * f * 4 + TM * d * 2 + TM * f * 2) + d * f * 2 + 2 * TM * f * 4
    return pl.pallas_call(
        _moe_up_kernel,
        grid_spec=pltpu.PrefetchScalarGridSpec(
            num_scalar_prefetch=2,
            grid=(tmax,),
            in_specs=[pl.BlockSpec((TM, d), lambda t, te, nt: (t, 0)),
                      pl.BlockSpec((TM, f), lambda t, te, nt: (t, 0)),
                      pl.BlockSpec((1, d, f), lambda t, te, nt: (e0 + te[t], 0, 0))],
            out_specs=pl.BlockSpec((TM, f), lambda t, te, nt: (t, 0)),
        ),
        out_shape=jax.ShapeDtypeStruct((tmax * TM, f), BF16),
        compiler_params=_params(1, vmem + (6 << 20)),
        name="moe_up",
    )(tile_expert, n_tiles, xs, g, w_up)


def _moe_down_kernel(te_ref, nt_ref, act_ref, wd_ref, y_ref):
    t = pl.program_id(0)

    @pl.when(t < nt_ref[0])
    def _():
        y_ref[...] = _pack_bf16_pairs(jnp.dot(act_ref[...], wd_ref[0].astype(BF16), preferred_element_type=F32))

    @pl.when(t >= nt_ref[0])
    def _():
        y_ref[...] = jnp.zeros(y_ref.shape, y_ref.dtype)


def moe_down(act, w_down, e0, tile_expert, n_tiles, TM):
    f = act.shape[1]
    d = w_down.shape[2]
    tmax = tile_expert.shape[0]
    vmem = 2 * (TM * f * 2 + f * d * 4 + TM * d * 2) + f * d * 2 + 3 * TM * d * 4
    return pl.pallas_call(
        _moe_down_kernel,
        grid_spec=pltpu.PrefetchScalarGridSpec(
            num_scalar_prefetch=2,
            grid=(tmax,),
            in_specs=[pl.BlockSpec((TM, f), lambda t, te, nt: (t, 0)),
                      pl.BlockSpec((1, f, d), lambda t, te, nt: (e0 + te[t], 0, 0))],
            out_specs=pl.BlockSpec((TM, d // 2), lambda t, te, nt: (t, 0)),
        ),
        out_shape=jax.ShapeDtypeStruct((tmax * TM, d // 2), jnp.uint32),
        compiler_params=_params(1, vmem + (8 << 20)),
        name="moe_down",
    )(tile_expert, n_tiles, act, w_down)


def _combine_ln_kernel(pos_ref, pos_next_ref, y_hbm, gate_ref, x_ref, g_ref, b_ref, xo_ref, xb_ref, ybuf, sem,
                       *, TC, alpha):
    i = pl.program_id(0)
    n = pl.num_programs(0)
    slot = i % 2
    rows = TOP_K * TC

    @pl.when(i == 0)
    def _():
        _gather_start(y_hbm, pos_ref, ybuf, 0, rows, sem)

    @pl.when(i + 1 < n)
    def _():
        _gather_start(y_hbm, pos_next_ref, ybuf, 1 - slot, rows, sem)

    _gather_wait(y_hbm, ybuf, slot, rows, sem)
    gate = gate_ref[...]
    ffn = gate[:, 0:1] * _unpack_bf16_pairs(ybuf[slot, 0:TC, :])
    for kk in range(1, TOP_K):
        ffn = ffn + gate[:, kk:kk + 1] * _unpack_bf16_pairs(ybuf[slot, kk * TC:(kk + 1) * TC, :])
    y = _layer_norm_rows(alpha * x_ref[...] + ffn, g_ref[...], b_ref[...])
    xo_ref[...] = y
    xb_ref[...] = y.astype(BF16)


def moe_combine_ln(y, pos, gates, x, g, b, alpha, TC):
    m, d = x.shape
    nt = m // TC
    kern = functools.partial(_combine_ln_kernel, TC=TC, alpha=alpha)
    vmem = 2 * (TC * LANES * 4 + 2 * TC * d * 4 + TC * d * 2) + 2 * TOP_K * TC * d * 4 + 3 * TC * d * 4
    return pl.pallas_call(
        kern,
        grid=(nt,),
        in_specs=[pl.BlockSpec((1, 1, TOP_K * TC), lambda i: (i, 0, 0), memory_space=pltpu.SMEM),
                  pl.BlockSpec((1, 1, TOP_K * TC), lambda i: (jnp.minimum(i + 1, nt - 1), 0, 0),
                               memory_space=pltpu.SMEM),
                  pl.BlockSpec(memory_space=pl.ANY),
                  pl.BlockSpec((TC, TOP_K), lambda i: (i, 0)),
                  pl.BlockSpec((TC, d), lambda i: (i, 0)),
                  pl.BlockSpec((1, d), lambda i: (0, 0)),
                  pl.BlockSpec((1, d), lambda i: (0, 0))],
        out_specs=[pl.BlockSpec((TC, d), lambda i: (i, 0)),
                   pl.BlockSpec((TC, d), lambda i: (i, 0))],
        out_shape=[jax.ShapeDtypeStruct((m, d), F32), jax.ShapeDtypeStruct((m, d), BF16)],
        scratch_shapes=[pltpu.VMEM((2, TOP_K * TC, d // 2), jnp.uint32), pltpu.SemaphoreType.DMA((2,))],
        compiler_params=_params(1, vmem + (8 << 20)),
        name="moe_combine_ln",
    )(pos, pos, y, gates, x, g, b)


def _routing_tables(eid, TM, TC):
    L = eid.shape[1]
    n_assign = TOP_K * L
    tmax = n_assign // TM + N_EXPERTS
    flat = eid.reshape(-1)
    onehot = (flat[:, None] == jnp.arange(N_EXPERTS, dtype=jnp.int32)[None, :]).astype(jnp.int32)
    csum = jnp.cumsum(onehot, axis=0)
    counts = csum[-1]
    rank = jnp.sum((csum - onehot) * onehot, axis=1)
    tiles_per = (counts + TM - 1) // TM
    tile_end = jnp.cumsum(tiles_per)
    tile_start = tile_end - tiles_per
    pos = tile_start[flat] * TM + rank
    token = jnp.tile(jnp.arange(L, dtype=jnp.int32), TOP_K)
    row_token = jnp.zeros((tmax * TM,), jnp.int32).at[pos].set(token)
    tile_ids = jnp.arange(tmax, dtype=jnp.int32)
    tile_expert = jnp.minimum(jnp.sum((tile_ids[:, None] >= tile_end[None, :]).astype(jnp.int32), axis=1),
                              N_EXPERTS - 1).astype(jnp.int32)
    n_tiles = tile_end[-1:].astype(jnp.int32)
    tile_expert = jnp.where(tile_ids < n_tiles[0], tile_expert, tile_expert[jnp.maximum(n_tiles[0] - 1, 0)])
    pos_tiles = pos.reshape(TOP_K, L // TC, TC).transpose(1, 0, 2).reshape(L // TC, 1, TOP_K * TC)
    return row_token.reshape(tmax, 1, TM), pos_tiles.astype(jnp.int32), tile_expert, n_tiles


def _moe_block(x, xp, eid, gates, w_gate, w_up, w_down, e0, ln_g, ln_b, alpha):
    L = x.shape[0]
    TM = min(256, L)
    TC = min(128, L)
    row_token, pos, tile_expert, n_tiles = _routing_tables(eid, TM, TC)
    g, xs = moe_gate(xp, w_gate, e0, tile_expert, n_tiles, row_token, TM)
    act = moe_up(xs, g, w_up, e0, tile_expert, n_tiles, TM)
    y = moe_down(act, w_down, e0, tile_expert, n_tiles, TM)
    return moe_combine_ln(y, pos, gates.T, x, ln_g, ln_b, alpha, TC)


def _ab_mixer(xb, w_in_all, w_out_all, li, sink, pe, w1, b1, w2, b2):
    L, d = xb.shape
    ha = d // (2 * HEAD_DIM)
    hb = d // (2 * HEAD_DIM)
    aq, akv, bq = ha * HEAD_DIM, A_KV_HEADS * HEAD_DIM, hb * HEAD_DIM
    bkv = N_BRANCH * 2 * HEAD_DIM
    ngate = hb * N_BRANCH
    s0, s2, s3, s4 = aq, aq + 2 * akv, aq + 2 * akv + bq, aq + 2 * akv + bq + bkv
    chunks = ((0, s0, Q_SCALE), (s2, bq, Q_SCALE), (s0, 2 * akv, 1.0), (s3, bkv, 1.0), (s4, ngate, 1.0))
    w = cast_weight(w_in_all, li, chunks, pad_to=512)
    proj = proj_matmul(xb, w, BF16)
    qa0, qb0 = 0, ha
    ka0 = ha + hb
    va0 = ka0 + A_KV_HEADS
    kvb0 = va0 + A_KV_HEADS
    gate_blk = kvb0 + 2 * N_BRANCH
    slopes_a = _alibi_slopes2(ha)
    slopes_b = _alibi_slopes2(hb)
    oa = banded_attention(proj, slopes_a, sink.astype(F32) * LOG2E, q_base=qa0, k_blk=ka0, v_blk=va0,
                          n_kv=A_KV_HEADS, G=ha // A_KV_HEADS, window=A_WINDOW, has_sink=True)
    ng = L // CMP_STRIDE
    n_cmp = (L - CMP_LEN) // CMP_STRIDE + 1
    n_sel = L // SEL_BLOCK
    nsp = -(-n_sel // LANES) * LANES
    groups = proj[kvb0:kvb0 + 2].reshape(2, ng, CMP_STRIDE * HEAD_DIM)
    kvc = nsa_compress(groups, pe.reshape(2, 1, CMP_LEN * HEAD_DIM).astype(BF16), w1.astype(BF16),
                       b1.reshape(2, 1, CMP_HIDDEN).astype(F32), w2.astype(BF16),
                       b2.reshape(2, 1, HEAD_DIM).astype(F32))
    cstart = np.arange(ng) * CMP_STRIDE
    sstart = np.arange(nsp) * SEL_BLOCK
    overlap = ((cstart[:, None] < sstart[None, :] + SEL_BLOCK) & (cstart[:, None] + CMP_LEN > sstart[None, :])
               & (np.arange(ng)[:, None] < n_cmp) & (np.arange(nsp)[None, :] < n_sel))
    overlap = jnp.asarray(overlap.astype(np.float32), dtype=BF16)
    ob, sel = nsa_cmp_attention(proj, kvc, slopes_b, overlap, q_base=qb0, H=hb, gate_blk=gate_blk,
                                n_cmp=n_cmp, n_sel=n_sel)
    kcols, qx, unsel = _sel_extra_columns(L, slopes_b, sel, n_sel)
    kx = jnp.concatenate([proj[kvb0 + 2], kcols], axis=1)
    vx = jnp.concatenate([proj[kvb0 + 3], jnp.ones((L, LANES), BF16)], axis=1)
    ob = nsa_sel_attention(proj, kx, vx, qx, unsel, ob, q_base=qb0, H=hb, gate_blk=gate_blk)
    ob = banded_attention(proj, slopes_b, jnp.zeros((hb,), F32), q_base=qb0, k_blk=kvb0 + 4, v_blk=kvb0 + 5,
                          n_kv=1, G=hb, window=B_WINDOW, gate_blk=gate_blk, gate_col=2, oacc=ob)
    return outproj([oa, ob], cast_weight(w_out_all, li))


def _c_mixer(xb, w_in_all, w_out_all, li):
    d = xb.shape[1]
    hc = d // HEAD_DIM
    qkv = proj_matmul(xb, w_in_all, F32, layer=li, n_q_cols=d)
    o = dilated_attention(qkv, _alibi_slopes2(hc), H=hc)
    return outproj([o], cast_weight(w_out_all, li))


def kernel(x, ab_w_in, ab_w_out, a_sink, nsa_cmp_pe, nsa_cmp_w1, nsa_cmp_b1, nsa_cmp_w2, nsa_cmp_b2, c_w_in, c_w_out, ln_mix_g, ln_mix_b, ln_ffn_g, ln_ffn_b, router_w, router_b, moe_w_gate, moe_w_up, moe_w_down):
    batch, L, d = x.shape
    depth = ln_mix_g.shape[0]
    alpha = float((2 * depth) ** 0.25)
    router_wt = router_w.T.astype(F32)
    router_bc = router_b.reshape(N_EXPERTS, 1).astype(F32)
    d_ff = moe_w_gate.shape[-1]
    wg_all = moe_w_gate.astype(F32).reshape(depth * N_EXPERTS, d, d_ff)
    wu_all = moe_w_up.astype(F32).reshape(depth * N_EXPERTS, d, d_ff)
    wd_all = moe_w_down.astype(F32).reshape(depth * N_EXPERTS, d_ff, d)
    outs = []
    for bi in range(batch):
        xf = x[bi].astype(F32)
        xb = xf.astype(BF16)
        for layer in range(depth):
            i = layer // 2
            g_mix, b_mix = ln_mix_g[layer].reshape(1, d), ln_mix_b[layer].reshape(1, d)
            g_ffn, b_ffn = ln_ffn_g[layer].reshape(1, d), ln_ffn_b[layer].reshape(1, d)
            if layer % 2 == 0:
                mix = _ab_mixer(xb, ab_w_in, ab_w_out, i, a_sink[i], nsa_cmp_pe[i], nsa_cmp_w1[i],
                                nsa_cmp_b1[i], nsa_cmp_w2[i], nsa_cmp_b2[i])
            else:
                mix = _c_mixer(xb, c_w_in, c_w_out, i)
            xf, xp, eid, gates = ln_router(mix, xf, g_mix, b_mix, router_wt, router_bc, alpha)
            xf, xb = _moe_block(xf, xp, eid, gates, wg_all, wu_all, wd_all, layer * N_EXPERTS, g_ffn, b_ffn,
                                alpha)
        outs.append(xf)
    return jnp.stack(outs, axis=0).astype(x.dtype)
```

```python
import functools
import math

import numpy as np
import jax
import jax.numpy as jnp
from jax import lax
from jax.experimental import pallas as pl
from jax.experimental.pallas import tpu as pltpu

HEAD_DIM = 128
BLK = 128
A_KV_HEADS = 2
A_WINDOW = 128
N_BRANCH = 3
CMP_LEN = 32
CMP_STRIDE = 16
CMP_HIDDEN = 256
SEL_BLOCK = 64
SEL_TOPK = 16
B_WINDOW = 512
C_PATTERNS = ((128, 1), (512, 4), (2048, 16))
N_EXPERTS = 16
N_GROUPS = 4
EXPERTS_PER_GROUP = N_EXPERTS // N_GROUPS
TOP_K = 2
LN_EPS = 1e-5
NEG = -1e30
M_INIT = -1e20
LOG2E = math.log2(math.e)
Q_SCALE = HEAD_DIM ** -0.5 * LOG2E

LANES = 128
VMEM_BUDGET = 56 * 1024 * 1024

F32 = jnp.float32
BF16 = jnp.bfloat16
NT_DIMS = (((1,), (1,)), ((), ()))


def _params(n_grid, vmem_bytes):
    return pltpu.CompilerParams(
        dimension_semantics=("arbitrary",) * n_grid,
        vmem_limit_bytes=int(min(VMEM_BUDGET, max(vmem_bytes, 16 * 1024 * 1024))),
    )


def _alibi_slopes2(n):
    return jnp.asarray(2.0 ** (-8.0 * np.arange(1, n + 1) / n) * LOG2E, dtype=F32)


def _layer_norm_rows(z, g, b):
    mu = jnp.mean(z, axis=-1, keepdims=True)
    zc = z - mu
    var = jnp.mean(zc * zc, axis=-1, keepdims=True)
    return zc * lax.rsqrt(var + LN_EPS) * g + b


HI_HALF = 0xFFFF0000


def _pack_bf16_pairs(x):
    n = x.shape[1] // 2
    lo = lax.bitcast_convert_type(x[:, :n].astype(BF16).astype(F32), jnp.uint32) >> 16
    hi = lax.bitcast_convert_type(x[:, n:].astype(BF16).astype(F32), jnp.uint32) & jnp.uint32(HI_HALF)
    return lo | hi


def _unpack_bf16_pairs(pk):
    lo = lax.bitcast_convert_type(pk << 16, F32)
    hi = lax.bitcast_convert_type(pk & jnp.uint32(HI_HALF), F32)
    return jnp.concatenate([lo, hi], axis=1)


def _lane_tile(x, n):
    return x if n == LANES else jnp.concatenate([x] * (n // LANES), axis=1)


def _cast_kernel(w_ref, o_ref, *, chunks, n_used):
    dst = 0
    for src, size, scale in chunks:
        blk = w_ref[0, :, src:src + size]
        if scale != 1.0:
            blk = blk * scale
        o_ref[:, dst:dst + size] = blk.astype(o_ref.dtype)
        dst += size
    assert dst == n_used
    if n_used < o_ref.shape[1]:
        o_ref[:, n_used:] = jnp.zeros((o_ref.shape[0], o_ref.shape[1] - n_used), o_ref.dtype)


def cast_weight(w_stacked, layer, chunks=None, pad_to=1):
    _, k, n = w_stacked.shape
    if chunks is None:
        chunks = ((0, n, 1.0),)
    n_used = sum(c[1] for c in chunks)
    n_out = -(-n_used // pad_to) * pad_to
    tr = 128 if n > 8192 else 256
    tr = min(tr, k)
    kern = functools.partial(_cast_kernel, chunks=tuple(chunks), n_used=n_used)
    vmem = 2 * (tr * n * 4 + tr * n_out * 2) + tr * n * 4
    return pl.pallas_call(
        kern,
        grid=(k // tr,),
        in_specs=[pl.BlockSpec((1, tr, n), lambda r: (layer, r, 0))],
        out_specs=pl.BlockSpec((tr, n_out), lambda r: (r, 0)),
        out_shape=jax.ShapeDtypeStruct((k, n_out), BF16),
        compiler_params=_params(1, vmem + (8 << 20)),
        name="cast_weight",
    )(w_stacked)


def _proj_kernel(x_ref, w_ref, o_ref, *, n_q_tiles):
    w = w_ref[...] if len(w_ref.shape) == 2 else w_ref[0].astype(BF16)
    acc = jnp.dot(x_ref[...], w, preferred_element_type=F32)
    if n_q_tiles:
        acc = acc * jnp.where(pl.program_id(1) < n_q_tiles, Q_SCALE, 1.0)
    for c in range(o_ref.shape[0]):
        o_ref[c] = acc[:, c * LANES:(c + 1) * LANES].astype(o_ref.dtype)


def proj_matmul(x, w, out_dtype, layer=None, n_q_cols=0):
    m, k = x.shape
    n = w.shape[-1]
    tm = min(1024, m)
    tn = min(512, n)
    assert n_q_cols % tn == 0
    osz = jnp.dtype(out_dtype).itemsize
    if layer is None:
        w_spec = pl.BlockSpec((k, tn), lambda i, j: (0, j))
        vmem = 2 * (tm * k * 2 + k * tn * 2 + tm * tn * osz) + 2 * tm * tn * 4
    else:
        w_spec = pl.BlockSpec((1, k, tn), lambda i, j: (layer, 0, j))
        vmem = 2 * (tm * k * 2 + k * tn * 4 + tm * tn * osz) + k * tn * 2 + 2 * tm * tn * 4
    return pl.pallas_call(
        functools.partial(_proj_kernel, n_q_tiles=n_q_cols // tn),
        grid=(m // tm, n // tn),
        in_specs=[pl.BlockSpec((tm, k), lambda i, j: (i, 0)), w_spec],
        out_specs=pl.BlockSpec((tn // LANES, tm, LANES), lambda i, j: (j, i, 0)),
        out_shape=jax.ShapeDtypeStruct((n // LANES, m, LANES), out_dtype),
        compiler_params=_params(2, vmem + (6 << 20)),
        name="proj_matmul",
    )(x, w)


def _banded_kernel(slopes_ref, sinks_ref, q_ref, kc_ref, kp_ref, vc_ref, vp_ref, *rest,
                   G, R, PR, window, has_sink, gate_col):
    if gate_col is not None:
        g_ref, oacc_ref, o_ref, kbuf, vbuf = rest
    else:
        o_ref, kbuf, vbuf = rest
    kv = pl.program_id(0)
    i = pl.program_id(1)
    kn = PR + BLK
    kbuf[0:PR, :] = kp_ref[0]
    kbuf[PR:PR + R, :] = kc_ref[0]
    vbuf[0:PR, :] = vp_ref[0]
    vbuf[PR:PR + R, :] = vc_ref[0]
    qi = lax.broadcasted_iota(jnp.int32, (BLK, kn), 0)
    kc = lax.broadcasted_iota(jnp.int32, (BLK, kn), 1)
    dist = PR + qi - kc
    static_valid = (dist >= 0) & (dist < window)
    negd = -dist.astype(F32)

    def unit(j, carry):
        row0 = pl.multiple_of(j * BLK, BLK)
        first_key = PR - (i * R + j * BLK)
        valid = static_valid & (kc >= first_key)
        k = kbuf[pl.ds(row0, kn), :]
        v = vbuf[pl.ds(row0, kn), :]
        q = q_ref[:, pl.ds(row0, BLK), :].reshape(G * BLK, HEAD_DIM)
        s_all = lax.dot_general(q, k, NT_DIMS, preferred_element_type=F32)
        ps, ls = [], []
        for g in range(G):
            h = kv * G + g
            s = s_all[g * BLK:(g + 1) * BLK] + slopes_ref[h] * negd
            s = jnp.where(valid, s, NEG)
            m = jnp.max(s, axis=-1, keepdims=True)
            p = jnp.exp2(s - m)
            l = jnp.sum(p, axis=-1, keepdims=True)
            if has_sink:
                l = l + jnp.exp2(sinks_ref[h] - m)
            ps.append(p.astype(BF16))
            ls.append(l)
        acc = jnp.dot(jnp.concatenate(ps, axis=0), v, preferred_element_type=F32)
        for g in range(G):
            o = acc[g * BLK:(g + 1) * BLK] / ls[g]
            if gate_col is not None:
                gl = g_ref[0, pl.ds(row0, BLK), :].astype(F32)
                c = g * N_BRANCH + gate_col
                o = oacc_ref[g, pl.ds(row0, BLK), :] + jax.nn.sigmoid(gl[:, c:c + 1]) * o
            o_ref[g, pl.ds(row0, BLK), :] = o.astype(o_ref.dtype)
        return carry

    lax.fori_loop(0, R // BLK, unit, 0)


def banded_attention(proj, slopes2, sinks2, *, q_base, k_blk, v_blk, n_kv, G, window,
                     has_sink=False, gate_blk=None, gate_col=None, oacc=None):
    L = proj.shape[1]
    n_prev = -(-(window - 1) // BLK)
    PR = n_prev * BLK
    R = max(PR, min(L, 8192 // G))
    assert L % R == 0 and R % PR == 0 and q_base % G == 0
    qb0 = q_base // G
    rp = R // PR
    in_specs = [
        pl.BlockSpec((G, R, LANES), lambda kv, i, *_: (qb0 + kv, i, 0)),
        pl.BlockSpec((1, R, LANES), lambda kv, i, *_: (k_blk + kv, i, 0)),
        pl.BlockSpec((1, PR, LANES), lambda kv, i, *_: (k_blk + kv, jnp.maximum(i * rp - 1, 0), 0)),
        pl.BlockSpec((1, R, LANES), lambda kv, i, *_: (v_blk + kv, i, 0)),
        pl.BlockSpec((1, PR, LANES), lambda kv, i, *_: (v_blk + kv, jnp.maximum(i * rp - 1, 0), 0)),
    ]
    args = [proj, proj, proj, proj, proj]
    vmem = 2 * (G * R * LANES * 2 * 2 + 4 * (R + PR) * LANES * 2) + 2 * (R + PR) * LANES * 2
    if gate_col is not None:
        assert n_kv == 1
        in_specs += [pl.BlockSpec((1, R, LANES), lambda kv, i, *_: (gate_blk, i, 0)),
                     pl.BlockSpec((G, R, LANES), lambda kv, i, *_: (kv, i, 0))]
        args += [proj, oacc]
        vmem += 2 * (R * LANES * 2 + G * R * LANES * 4)
    kern = functools.partial(_banded_kernel, G=G, R=R, PR=PR, window=window,
                             has_sink=has_sink, gate_col=gate_col)
    return pl.pallas_call(
        kern,
        grid_spec=pltpu.PrefetchScalarGridSpec(
            num_scalar_prefetch=2,
            grid=(n_kv, L // R),
            in_specs=in_specs,
            out_specs=pl.BlockSpec((G, R, LANES), lambda kv, i, *_: (kv, i, 0)),
            scratch_shapes=[pltpu.VMEM((PR + R, LANES), BF16), pltpu.VMEM((PR + R, LANES), BF16)],
        ),
        out_shape=jax.ShapeDtypeStruct((n_kv * G, L, LANES), BF16),
        compiler_params=_params(2, vmem + (16 << 20)),
        name="banded_attention",
    )(slopes2, sinks2, *args)


def _compress_kernel(g_ref, pe_ref, w1_ref, b1_ref, w2_ref, b2_ref, o_ref):
    ng = g_ref.shape[1]
    half = w1_ref.shape[1] // 2
    g = g_ref[0]
    y0 = jnp.dot(g, w1_ref[0, 0:half, :], preferred_element_type=F32)
    y1 = jnp.dot(g, w1_ref[0, half:, :], preferred_element_type=F32)
    y1 = pltpu.roll(y1, ng - 1, 0)
    pe = jnp.broadcast_to(pe_ref[0], (8, pe_ref.shape[2]))
    ype = jnp.dot(pe, w1_ref[0], preferred_element_type=F32)[0:1]
    h = jax.nn.gelu(y0 + y1 + ype + b1_ref[0])
    out = jnp.dot(h.astype(BF16), w2_ref[0], preferred_element_type=F32) + b2_ref[0]
    o_ref[0] = out.astype(o_ref.dtype)


def nsa_compress(groups, pe, w1, b1, w2, b2):
    _, ng, gw = groups.shape
    hid = w1.shape[2]
    vmem = 2 * (ng * gw * 2 + 2 * gw * hid * 2 + 2 * gw * 2) + 6 * ng * hid * 4
    return pl.pallas_call(
        _compress_kernel,
        grid=(2,),
        in_specs=[pl.BlockSpec((1, ng, gw), lambda c: (c, 0, 0)),
                  pl.BlockSpec((1, 1, 2 * gw), lambda c: (c, 0, 0)),
                  pl.BlockSpec((1, 2 * gw, hid), lambda c: (c, 0, 0)),
                  pl.BlockSpec((1, 1, hid), lambda c: (c, 0, 0)),
                  pl.BlockSpec((1, hid, HEAD_DIM), lambda c: (c, 0, 0)),
                  pl.BlockSpec((1, 1, HEAD_DIM), lambda c: (c, 0, 0))],
        out_specs=pl.BlockSpec((1, ng, HEAD_DIM), lambda c: (c, 0, 0)),
        out_shape=jax.ShapeDtypeStruct((2, ng, HEAD_DIM), BF16),
        compiler_params=_params(1, vmem + (8 << 20)),
        name="nsa_compress",
    )(groups, pe, w1, b1, w2, b2)


def _cmp_kernel(slopes_ref, q_ref, kv_ref, g_ref, ov_ref, o_ref, sel_ref, *, H, TQ, n_cmp, n_sel):
    qb = pl.program_id(0)
    ncp = kv_ref.shape[1]
    nsp = sel_ref.shape[1]
    kc = kv_ref[0]
    vc = kv_ref[1]
    t = qb * TQ + lax.broadcasted_iota(jnp.int32, (TQ, 1), 0)
    n = lax.broadcasted_iota(jnp.int32, (1, ncp), 1)
    cdist = t - (n * CMP_STRIDE + (CMP_LEN - 1))
    valid = (cdist >= 0) & (n < n_cmp)
    negd = -jnp.maximum(cdist, 0).astype(F32)
    gl = g_ref[0].astype(F32)
    psum = jnp.zeros((TQ, ncp), F32)
    for h in range(H):
        s = lax.dot_general(q_ref[h], kc, NT_DIMS, preferred_element_type=F32)
        s = jnp.where(valid, s + slopes_ref[h] * negd, NEG)
        m = jnp.max(s, axis=-1, keepdims=True)
        e = jnp.exp2(s - m)
        l = jnp.sum(e, axis=-1, keepdims=True)
        p = jnp.where(valid, e / l, 0.0)
        psum = psum + p
        o = jnp.dot(p.astype(BF16), vc, preferred_element_type=F32)
        c = h * N_BRANCH
        o_ref[h] = jax.nn.sigmoid(gl[:, c:c + 1]) * o
    p_hi = psum.astype(BF16)
    p_lo = (psum - p_hi.astype(F32)).astype(BF16)
    imp = (jnp.dot(p_hi, ov_ref[...], preferred_element_type=F32)
           + jnp.dot(p_lo, ov_ref[...], preferred_element_type=F32))
    j = lax.broadcasted_iota(jnp.int32, (1, nsp), 1)
    jf = j.astype(F32)
    cur = t // SEL_BLOCK
    forced = (j == 0) | (j == cur) | (j == cur - 1)
    svalid = j * SEL_BLOCK <= t
    work = jnp.where(forced, 1e30, jnp.where(svalid, imp, -1.0))
    work = jnp.where(j < n_sel, work, -2.0)
    sel = jnp.zeros((TQ, nsp), F32)
    for _ in range(min(SEL_TOPK, n_sel)):
        mx = jnp.max(work, axis=-1, keepdims=True)
        idx = jnp.min(jnp.where(work == mx, jf, 1e9), axis=-1, keepdims=True)
        hit = jf == idx
        sel = jnp.where(hit, 1.0, sel)
        work = jnp.where(hit, -2.0, work)
    sel_ref[...] = sel.astype(sel_ref.dtype)


def nsa_cmp_attention(proj, kvc, slopes2, overlap, *, q_base, H, gate_blk, n_cmp, n_sel):
    L = proj.shape[1]
    ncp = kvc.shape[1]
    nsp = overlap.shape[1]
    TQ = min(256, L)
    kern = functools.partial(_cmp_kernel, H=H, TQ=TQ, n_cmp=n_cmp, n_sel=n_sel)
    vmem = 2 * (H * TQ * LANES * 2 + 2 * ncp * LANES * 2 + TQ * LANES * 2 + ncp * nsp * 2
                + H * TQ * LANES * 4 + TQ * nsp * 2) + 8 * TQ * ncp * 4
    return pl.pallas_call(
        kern,
        grid_spec=pltpu.PrefetchScalarGridSpec(
            num_scalar_prefetch=1,
            grid=(L // TQ,),
            in_specs=[pl.BlockSpec((H, TQ, LANES), lambda qb, *_: (q_base // H, qb, 0)),
                      pl.BlockSpec((2, ncp, LANES), lambda qb, *_: (0, 0, 0)),
                      pl.BlockSpec((1, TQ, LANES), lambda qb, *_: (gate_blk, qb, 0)),
                      pl.BlockSpec((ncp, nsp), lambda qb, *_: (0, 0))],
            out_specs=[pl.BlockSpec((H, TQ, LANES), lambda qb, *_: (0, qb, 0)),
                       pl.BlockSpec((TQ, nsp), lambda qb, *_: (qb, 0))],
        ),
        out_shape=[jax.ShapeDtypeStruct((H, L, LANES), F32),
                   jax.ShapeDtypeStruct((L, nsp), BF16)],
        compiler_params=_params(1, vmem + (8 << 20)),
        name="nsa_cmp_attention",
    )(slopes2, proj, kvc, proj, overlap)


def _sel_kernel(q_ref, qx_ref, k_ref, v_ref, unsel_ref, g_ref, oacc_ref, o_ref, qs, m_s, acc_s, *, H, TQ, TK):
    qb = pl.program_id(0)
    kb = pl.program_id(1)
    nkb = pl.num_programs(1)

    @pl.when(kb == 0)
    def _():
        for h in range(H):
            qs[h * TQ:(h + 1) * TQ, 0:LANES] = q_ref[h]
        m_s[...] = jnp.full(m_s.shape, M_INIT, F32)
        acc_s[...] = jnp.zeros(acc_s.shape, F32)

    def step(causal):
        for h in range(H):
            qs[h * TQ:(h + 1) * TQ, LANES:2 * LANES] = qx_ref[h] + unsel_ref[0]
        s_all = lax.dot_general(qs[...], k_ref[...], NT_DIMS, preferred_element_type=F32)
        if causal:
            t = qb * TQ + lax.broadcasted_iota(jnp.int32, (TQ, 1), 0)
            kk = kb * TK + lax.broadcasted_iota(jnp.int32, (1, TK), 1)
            cmask = kk <= t
        ps = []
        for h in range(H):
            rows = slice(h * TQ, (h + 1) * TQ)
            s = s_all[rows]
            if causal:
                s = jnp.where(cmask, s, NEG)
            m_old = m_s[rows, :]
            m_new = jnp.maximum(m_old, jnp.max(s, axis=-1, keepdims=True))
            m_s[rows, :] = m_new
            alpha = jnp.exp2(m_old - m_new)
            acc_s[rows, :] = acc_s[rows, :] * _lane_tile(alpha, 2 * LANES)
            ps.append(jnp.exp2(s - _lane_tile(m_new, TK)).astype(BF16))
        acc_s[...] += jnp.dot(jnp.concatenate(ps, axis=0), v_ref[...], preferred_element_type=F32)

    last_key = kb * TK + (TK - 1)

    @pl.when(last_key <= qb * TQ)
    def _():
        step(False)

    @pl.when((last_key > qb * TQ) & (kb * TK <= qb * TQ + (TQ - 1)))
    def _():
        step(True)

    @pl.when(kb == nkb - 1)
    def _():
        gl = g_ref[0].astype(F32)
        for h in range(H):
            rows = slice(h * TQ, (h + 1) * TQ)
            c = h * N_BRANCH + 1
            o = acc_s[rows, 0:LANES] / acc_s[rows, LANES:2 * LANES]
            o_ref[h] = oacc_ref[h] + jax.nn.sigmoid(gl[:, c:c + 1]) * o


SEL_TQ = 256
SEL_TK = 1024
N_POS_COLS = 6
UNSEL_LOGIT = -(2.0 ** 100)


def nsa_sel_attention(proj, kx, vx, qx, unsel, oacc, *, q_base, H, gate_blk):
    L = proj.shape[1]
    TQ = min(SEL_TQ, L)
    TK = min(SEL_TK, L)

    def last_kb(qb):
        return (qb * TQ + TQ - 1) // TK

    kern = functools.partial(_sel_kernel, H=H, TQ=TQ, TK=TK)
    vmem = (2 * (H * TQ * LANES * 2 + 2 * TK * 2 * LANES * 2 + 2 * TQ * LANES * 2
                 + 2 * H * TQ * LANES * 4) + H * TQ * LANES * (4 + 4 + 8)
            + H * TQ * TK * (4 + 4 + 2))
    return pl.pallas_call(
        kern,
        grid=(L // TQ, L // TK),
        in_specs=[pl.BlockSpec((H, TQ, LANES), lambda qb, kb: (q_base // H, qb, 0)),
                  pl.BlockSpec((H, 1, LANES), lambda qb, kb: (0, 0, 0)),
                  pl.BlockSpec((TK, 2 * LANES), lambda qb, kb: (jnp.minimum(kb, last_kb(qb)), 0)),
                  pl.BlockSpec((TK, 2 * LANES), lambda qb, kb: (jnp.minimum(kb, last_kb(qb)), 0)),
                  pl.BlockSpec((1, TQ, LANES), lambda qb, kb: (jnp.minimum(kb, last_kb(qb)), qb, 0)),
                  pl.BlockSpec((1, TQ, LANES), lambda qb, kb: (gate_blk, qb, 0)),
                  pl.BlockSpec((H, TQ, LANES), lambda qb, kb: (0, qb, 0))],
        out_specs=pl.BlockSpec((H, TQ, LANES), lambda qb, kb: (0, qb, 0)),
        scratch_shapes=[pltpu.VMEM((H * TQ, 2 * LANES), BF16), pltpu.VMEM((H * TQ, LANES), F32),
                        pltpu.VMEM((H * TQ, 2 * LANES), F32)],
        out_shape=jax.ShapeDtypeStruct((H, L, LANES), F32),
        compiler_params=_params(2, vmem + (4 << 20)),
        name="nsa_sel_attention",
    )(proj, qx, kx, vx, unsel, proj, oacc)


def _sel_extra_columns(L, slopes2, sel, n_sel):
    TK = min(SEL_TK, L)
    nj = TK // SEL_BLOCK
    pos = np.arange(L)
    kcols = np.zeros((L, LANES), np.float32)
    kcols[:, 0:3] = ((pos // LANES) * LANES)[:, None]
    kcols[:, 3:6] = (pos % LANES)[:, None]
    kcols[pos, N_POS_COLS + (pos // SEL_BLOCK) % nj] = UNSEL_LOGIT
    a1 = slopes2.astype(BF16)
    r1 = slopes2 - a1.astype(F32)
    a2 = r1.astype(BF16)
    a3 = (r1 - a2.astype(F32)).astype(BF16)
    parts = jnp.stack([a1, a2, a3, a1, a2, a3], axis=1)
    qx = jnp.zeros((slopes2.shape[0], LANES), BF16).at[:, 0:N_POS_COLS].set(parts)
    flags = (1.0 - sel[:, :n_sel].astype(F32)).reshape(L, L // TK, nj).transpose(1, 0, 2)
    unsel = jnp.pad(flags, ((0, 0), (0, 0), (N_POS_COLS, LANES - N_POS_COLS - nj))).astype(BF16)
    return jnp.asarray(kcols, dtype=BF16), qx.reshape(-1, 1, LANES), unsel


def _dilated_kernel(slopes_ref, q_ref, k_ref, v_ref, o_ref, q4, k4, v4, ktail, vtail, o_s, l_s, *, CH):
    h = pl.program_id(0)
    i = pl.program_id(1)
    par = i % 2
    slope = slopes_ref[h]
    sub = CH // 4

    @pl.when(i == 0)
    def _():
        k4[1] = jnp.zeros(k4.shape[1:], F32)
        v4[1] = jnp.zeros(v4.shape[1:], F32)
        ktail[...] = jnp.zeros(ktail.shape, F32)
        vtail[...] = jnp.zeros(vtail.shape, F32)

    for r in range(4):
        q4[r] = q_ref[0, pl.ds(r, sub, stride=4), :]
        k4[par, r] = k_ref[0, pl.ds(r, sub, stride=4), :]
        v4[par, r] = v_ref[0, pl.ds(r, sub, stride=4), :]

    kn = 2 * BLK
    qi = lax.broadcasted_iota(jnp.int32, (BLK, kn), 0)
    kc = lax.broadcasted_iota(jnp.int32, (BLK, kn), 1)
    dist = BLK + qi - kc
    static_valid = (dist >= 0) & (dist <= BLK)
    first_valid = static_valid & (kc >= jnp.where(i == 0, BLK, 0))
    negd = -dist.astype(F32)
    ones = jnp.ones((kn, LANES), BF16)

    def unit(p, bias, q, kprev, kcur, vprev, vcur, rows):
        kk = jnp.concatenate([kprev.astype(BF16), kcur.astype(BF16)], axis=0)
        vv = jnp.concatenate([vprev.astype(BF16), vcur.astype(BF16)], axis=0)
        s = lax.dot_general(q.astype(BF16), kk, NT_DIMS, preferred_element_type=F32) + bias
        m = jnp.max(s, axis=-1, keepdims=True)
        e = jnp.exp2(s - m).astype(BF16)
        acc = jnp.dot(e, jnp.concatenate([vv, ones], axis=1), preferred_element_type=F32)
        l = acc[:, LANES:]
        o_s[p, rows, :] = acc[:, :LANES] / l
        l_s[p, rows, :] = m + jnp.log2(l)

    for p, (window, dil) in enumerate(C_PATTERNS):
        assert window // dil == BLK
        ab = (slope * dil) * negd
        bias = jnp.where(static_valid, ab, NEG)
        bias0 = jnp.where(first_valid, ab, NEG)
        if dil == 1:
            for j in range(CH // BLK):
                cur = slice(j * BLK, (j + 1) * BLK)
                prev = slice((j - 1) * BLK, j * BLK)
                unit(p, bias if j else bias0, q_ref[0, cur, :],
                     k_ref[0, prev, :] if j else ktail[...], k_ref[0, cur, :],
                     v_ref[0, prev, :] if j else vtail[...], v_ref[0, cur, :], pl.ds(j * BLK, BLK))
        elif dil == 4:
            for r in range(4):
                for j in range(sub // BLK):
                    cur = slice(j * BLK, (j + 1) * BLK)
                    prev = slice((j - 1) * BLK, j * BLK)
                    last = slice(sub - BLK, sub)
                    unit(p, bias if j else bias0, q4[r, cur, :],
                         k4[par, r, prev, :] if j else k4[1 - par, r, last, :], k4[par, r, cur, :],
                         v4[par, r, prev, :] if j else v4[1 - par, r, last, :], v4[par, r, cur, :],
                         pl.ds(r + 4 * j * BLK, BLK, stride=4))
        else:
            assert dil == 16 and sub == 4 * BLK
            for r in range(16):
                r4, a = r % 4, r // 4
                rows4 = pl.ds(a, BLK, stride=4)
                unit(p, bias0, q4[r4, rows4, :],
                     k4[1 - par, r4, rows4, :], k4[par, r4, rows4, :],
                     v4[1 - par, r4, rows4, :], v4[par, r4, rows4, :], pl.ds(r, BLK, stride=16))

    ktail[...] = k_ref[0, CH - BLK:CH, :]
    vtail[...] = v_ref[0, CH - BLK:CH, :]
    lses = [l_s[p] for p in range(len(C_PATTERNS))]
    mx = functools.reduce(jnp.maximum, lses)
    ws = [jnp.exp2(x - mx) for x in lses]
    num = functools.reduce(lambda a, b: a + b, [w * o_s[p] for p, w in enumerate(ws)])
    den = functools.reduce(lambda a, b: a + b, ws)
    o_ref[0] = (num / den).astype(o_ref.dtype)


def dilated_attention(qkv, slopes2, *, H):
    L = qkv.shape[1]
    CH = max(d for _, d in C_PATTERNS) * BLK
    assert L % CH == 0
    kern = functools.partial(_dilated_kernel, CH=CH)
    npat = len(C_PATTERNS)
    vmem = 2 * (3 * CH * LANES * 4 + CH * LANES * 2) + (5 + 2 * npat) * CH * LANES * 4
    return pl.pallas_call(
        kern,
        grid_spec=pltpu.PrefetchScalarGridSpec(
            num_scalar_prefetch=1,
            grid=(H, L // CH),
            in_specs=[pl.BlockSpec((1, CH, LANES), lambda h, i, *_: (h, i, 0)),
                      pl.BlockSpec((1, CH, LANES), lambda h, i, *_: (H + h, i, 0)),
                      pl.BlockSpec((1, CH, LANES), lambda h, i, *_: (2 * H + h, i, 0))],
            out_specs=pl.BlockSpec((1, CH, LANES), lambda h, i, *_: (h, i, 0)),
            scratch_shapes=[pltpu.VMEM((4, CH // 4, LANES), F32),
                            pltpu.VMEM((2, 4, CH // 4, LANES), F32),
                            pltpu.VMEM((2, 4, CH // 4, LANES), F32),
                            pltpu.VMEM((BLK, LANES), F32), pltpu.VMEM((BLK, LANES), F32),
                            pltpu.VMEM((npat, CH, LANES), F32), pltpu.VMEM((npat, CH, LANES), F32)],
        ),
        out_shape=jax.ShapeDtypeStruct((H, L, LANES), BF16),
        compiler_params=_params(2, vmem + (16 << 20)),
        name="dilated_attention",
    )(slopes2, qkv, qkv, qkv)


def _outproj_kernel(*refs):
    *o_refs, w_ref, y_ref = refs
    lhs = jnp.concatenate([o_ref[c] for o_ref in o_refs for c in range(o_ref.shape[0])], axis=1)
    y_ref[...] = jnp.dot(lhs, w_ref[...], preferred_element_type=F32)


def outproj(o_parts, w):
    m = o_parts[0].shape[1]
    k, n = w.shape
    assert k == sum(o.shape[0] for o in o_parts) * LANES
    tm = min(1024, m)
    tn = min(512, n)
    vmem = 2 * (tm * k * 2 + k * tn * 2 + tm * tn * 4) + tm * k * 2 + tm * tn * 4
    return pl.pallas_call(
        _outproj_kernel,
        grid=(m // tm, n // tn),
        in_specs=[pl.BlockSpec((o.shape[0], tm, LANES), lambda i, j: (0, i, 0)) for o in o_parts]
        + [pl.BlockSpec((k, tn), lambda i, j: (0, j))],
        out_specs=pl.BlockSpec((tm, tn), lambda i, j: (i, j)),
        out_shape=jax.ShapeDtypeStruct((m, n), F32),
        compiler_params=_params(2, vmem + (8 << 20)),
        name="outproj",
    )(*o_parts, w)


def _route(x, wt, b):
    logits = lax.dot_general(wt, x, NT_DIMS, preferred_element_type=F32,
                             precision=lax.Precision.HIGHEST) + b
    mx = jnp.max(logits, axis=0, keepdims=True)
    e = jnp.exp(logits - mx)
    probs = e / jnp.sum(e, axis=0, keepdims=True)
    rows = [probs[i:i + 1, :] for i in range(N_EXPERTS)]
    epg = EXPERTS_PER_GROUP
    best = None
    gsel = None
    for gi in range(N_GROUPS):
        grp = rows[gi * epg:(gi + 1) * epg]
        score = None
        for a in range(epg):
            for c in range(a + 1, epg):
                pair = grp[a] + grp[c]
                score = pair if score is None else jnp.maximum(score, pair)
        if best is None:
            best, gsel = score, jnp.zeros(score.shape, jnp.int32)
        else:
            better = score > best
            gsel = jnp.where(better, gi, gsel)
            best = jnp.maximum(best, score)
    ing = []
    for kk in range(epg):
        val = rows[kk]
        for gi in range(1, N_GROUPS):
            val = jnp.where(gsel == gi, rows[gi * epg + kk], val)
        ing.append(val)

    def argmax_first(vals):
        bv, bi = vals[0], jnp.zeros(vals[0].shape, jnp.int32)
        for kk in range(1, len(vals)):
            better = vals[kk] > bv
            bi = jnp.where(better, kk, bi)
            bv = jnp.maximum(bv, vals[kk])
        return bv, bi

    v1, i1 = argmax_first(ing)
    rest = [jnp.where(i1 == kk, -1.0, ing[kk]) for kk in range(epg)]
    v2, i2 = argmax_first(rest)
    tot = v1 + v2
    return (gsel * epg + i1, gsel * epg + i2), (v1 / tot, v2 / tot)


def _ln_router_kernel(y_ref, x_ref, g_ref, b_ref, wt_ref, rb_ref, xo_ref, xp_ref, eid_ref, gate_ref, *, alpha):
    xn = _layer_norm_rows(alpha * x_ref[...] + y_ref[...], g_ref[...], b_ref[...])
    xo_ref[...] = xn
    xp_ref[...] = _pack_bf16_pairs(xn)
    eids, gates = _route(xn, wt_ref[...], rb_ref[...])
    for kk in range(TOP_K):
        eid_ref[kk:kk + 1, :] = eids[kk]
        gate_ref[kk:kk + 1, :] = gates[kk]


def ln_router(y, x, g, b, wt, rb, alpha):
    m, d = x.shape
    tr = min(256, m)
    kern = functools.partial(_ln_router_kernel, alpha=alpha)
    vmem = 2 * (3 * tr * d * 4 + tr * d * 2 + N_EXPERTS * d * 4) + 8 * tr * d * 2 + 3 * tr * d * 4
    return pl.pallas_call(
        kern,
        grid=(m // tr,),
        in_specs=[pl.BlockSpec((tr, d), lambda i: (i, 0)),
                  pl.BlockSpec((tr, d), lambda i: (i, 0)),
                  pl.BlockSpec((1, d), lambda i: (0, 0)),
                  pl.BlockSpec((1, d), lambda i: (0, 0)),
                  pl.BlockSpec((N_EXPERTS, d), lambda i: (0, 0)),
                  pl.BlockSpec((N_EXPERTS, 1), lambda i: (0, 0))],
        out_specs=[pl.BlockSpec((tr, d), lambda i: (i, 0)),
                   pl.BlockSpec((tr, d // 2), lambda i: (i, 0)),
                   pl.BlockSpec((TOP_K, tr), lambda i: (0, i)),
                   pl.BlockSpec((TOP_K, tr), lambda i: (0, i))],
        out_shape=[jax.ShapeDtypeStruct((m, d), F32), jax.ShapeDtypeStruct((m, d // 2), jnp.uint32),
                   jax.ShapeDtypeStruct((TOP_K, m), jnp.int32), jax.ShapeDtypeStruct((TOP_K, m), F32)],
        compiler_params=_params(1, vmem + (4 << 20)),
        name="ln_router",
    )(y, x, g, b, wt, rb)


def _row_copy(src_hbm, row, dst, slot, r, sem):
    return pltpu.make_async_copy(src_hbm.at[pl.ds(row, 1)], dst.at[slot, pl.ds(r, 1)], sem.at[slot])


def _gather_start(src_hbm, idx_ref, dst, slot, n_rows, sem):
    def body(r, c):
        _row_copy(src_hbm, idx_ref[0, 0, r], dst, slot, r, sem).start()
        return c
    lax.fori_loop(0, n_rows, body, 0, unroll=8)


def _gather_wait(src_hbm, dst, slot, n_rows, sem):
    pltpu.make_async_copy(src_hbm.at[pl.ds(0, n_rows)], dst.at[slot], sem.at[slot]).wait()


def _moe_gate_kernel(te_ref, nt_ref, idx_ref, idx_next_ref, x_hbm, wg_ref, g_ref, xs_ref, xbuf, sem, *, TM):
    t = pl.program_id(0)
    nt = nt_ref[0]
    slot = t % 2

    @pl.when((t == 0) & (nt > 0))
    def _():
        _gather_start(x_hbm, idx_ref, xbuf, 0, TM, sem)

    @pl.when(t + 1 < nt)
    def _():
        _gather_start(x_hbm, idx_next_ref, xbuf, 1 - slot, TM, sem)

    @pl.when(t < nt)
    def _():
        _gather_wait(x_hbm, xbuf, slot, TM, sem)
        xb = _unpack_bf16_pairs(xbuf[slot]).astype(BF16)
        xs_ref[...] = xb
        g_ref[...] = jax.nn.silu(jnp.dot(xb, wg_ref[0].astype(BF16), preferred_element_type=F32))

    @pl.when(t >= nt)
    def _():
        xs_ref[...] = jnp.zeros(xs_ref.shape, xs_ref.dtype)
        g_ref[...] = jnp.zeros(g_ref.shape, g_ref.dtype)


def moe_gate(xp, w_gate, e0, tile_expert, n_tiles, row_token, TM):
    dp = xp.shape[1]
    d = 2 * dp
    f = w_gate.shape[2]
    tmax = tile_expert.shape[0]
    kern = functools.partial(_moe_gate_kernel, TM=TM)
    vmem = 2 * (d * f * 4 + TM * f * 4 + TM * d * 2) + d * f * 2 + 2 * TM * dp * 4 + 3 * TM * d * 4 + 2 * TM * f * 4
    return pl.pallas_call(
        kern,
        grid_spec=pltpu.PrefetchScalarGridSpec(
            num_scalar_prefetch=2,
            grid=(tmax,),
            in_specs=[pl.BlockSpec((1, 1, TM), lambda t, te, nt: (t, 0, 0), memory_space=pltpu.SMEM),
                      pl.BlockSpec((1, 1, TM), lambda t, te, nt: (jnp.minimum(t + 1, tmax - 1), 0, 0),
                                   memory_space=pltpu.SMEM),
                      pl.BlockSpec(memory_space=pl.ANY),
                      pl.BlockSpec((1, d, f), lambda t, te, nt: (e0 + te[t], 0, 0))],
            out_specs=[pl.BlockSpec((TM, f), lambda t, te, nt: (t, 0)),
                       pl.BlockSpec((TM, d), lambda t, te, nt: (t, 0))],
            scratch_shapes=[pltpu.VMEM((2, TM, dp), jnp.uint32), pltpu.SemaphoreType.DMA((2,))],
        ),
        out_shape=[jax.ShapeDtypeStruct((tmax * TM, f), F32), jax.ShapeDtypeStruct((tmax * TM, d), BF16)],
        compiler_params=_params(1, vmem + (6 << 20)),
        name="moe_gate",
    )(tile_expert, n_tiles, row_token, row_token, xp, w_gate)


def _moe_up_kernel(te_ref, nt_ref, xs_ref, g_ref, wu_ref, act_ref):
    t = pl.program_id(0)

    @pl.when(t < nt_ref[0])
    def _():
        hu = jnp.dot(xs_ref[...], wu_ref[0].astype(BF16), preferred_element_type=F32)
        act_ref[...] = (g_ref[...] * hu).astype(act_ref.dtype)

    @pl.when(t >= nt_ref[0])
    def _():
        act_ref[...] = jnp.zeros(act_ref.shape, act_ref.dtype)


def moe_up(xs, g, w_up, e0, tile_expert, n_tiles, TM):
    d = xs.shape[1]
    f = w_up.shape[2]
    tmax = tile_expert.shape[0]
    vmem = 2 * (d * f * 4 + TM * f * 4 + TM * d * 2 + TM * f * 2) + d * f * 2 + 2 * TM * f * 4
    return pl.pallas_call(
        _moe_up_kernel,
        grid_spec=pltpu.PrefetchScalarGridSpec(
            num_scalar_prefetch=2,
            grid=(tmax,),
            in_specs=[pl.BlockSpec((TM, d), lambda t, te, nt: (t, 0)),
                      pl.BlockSpec((TM, f), lambda t, te, nt: (t, 0)),
                      pl.BlockSpec((1, d, f), lambda t, te, nt: (e0 + te[t], 0, 0))],
            out_specs=pl.BlockSpec((TM, f), lambda t, te, nt: (t, 0)),
        ),
        out_shape=jax.ShapeDtypeStruct((tmax * TM, f), BF16),
        compiler_params=_params(1, vmem + (6 << 20)),
        name="moe_up",
    )(tile_expert, n_tiles, xs, g, w_up)


def _moe_down_kernel(te_ref, nt_ref, act_ref, wd_ref, y_ref):
    t = pl.program_id(0)

    @pl.when(t < nt_ref[0])
    def _():
        y_ref[...] = _pack_bf16_pairs(jnp.dot(act_ref[...], wd_ref[0].astype(BF16), preferred_element_type=F32))

    @pl.when(t >= nt_ref[0])
    def _():
        y_ref[...] = jnp.zeros(y_ref.shape, y_ref.dtype)


def moe_down(act, w_down, e0, tile_expert, n_tiles, TM):
    f = act.shape[1]
    d = w_down.shape[2]
    tmax = tile_expert.shape[0]
    vmem = 2 * (TM * f * 2 + f * d * 4 + TM * d * 2) + f * d * 2 + 3 * TM * d * 4
    return pl.pallas_call(
        _moe_down_kernel,
        grid_spec=pltpu.PrefetchScalarGridSpec(
            num_scalar_prefetch=2,
            grid=(tmax,),
            in_specs=[pl.BlockSpec((TM, f), lambda t, te, nt: (t, 0)),
                      pl.BlockSpec((1, f, d), lambda t, te, nt: (e0 + te[t], 0, 0))],
            out_specs=pl.BlockSpec((TM, d // 2), lambda t, te, nt: (t, 0)),
        ),
        out_shape=jax.ShapeDtypeStruct((tmax * TM, d // 2), jnp.uint32),
        compiler_params=_params(1, vmem + (8 << 20)),
        name="moe_down",
    )(tile_expert, n_tiles, act, w_down)


def _combine_ln_kernel(pos_ref, pos_next_ref, y_hbm, gate_ref, x_ref, g_ref, b_ref, xo_ref, xb_ref, ybuf, sem,
                       *, TC, alpha):
    i = pl.program_id(0)
    n = pl.num_programs(0)
    slot = i % 2
    rows = TOP_K * TC

    @pl.when(i == 0)
    def _():
        _gather_start(y_hbm, pos_ref, ybuf, 0, rows, sem)

    @pl.when(i + 1 < n)
    def _():
        _gather_start(y_hbm, pos_next_ref, ybuf, 1 - slot, rows, sem)

    _gather_wait(y_hbm, ybuf, slot, rows, sem)
    gate = gate_ref[...]
    ffn = gate[:, 0:1] * _unpack_bf16_pairs(ybuf[slot, 0:TC, :])
    for kk in range(1, TOP_K):
        ffn = ffn + gate[:, kk:kk + 1] * _unpack_bf16_pairs(ybuf[slot, kk * TC:(kk + 1) * TC, :])
    y = _layer_norm_rows(alpha * x_ref[...] + ffn, g_ref[...], b_ref[...])
    xo_ref[...] = y
    xb_ref[...] = y.astype(BF16)


def moe_combine_ln(y, pos, gates, x, g, b, alpha, TC):
    m, d = x.shape
    nt = m // TC
    kern = functools.partial(_combine_ln_kernel, TC=TC, alpha=alpha)
    vmem = 2 * (TC * LANES * 4 + 2 * TC * d * 4 + TC * d * 2) + 2 * TOP_K * TC * d * 4 + 3 * TC * d * 4
    return pl.pallas_call(
        kern,
        grid=(nt,),
        in_specs=[pl.BlockSpec((1, 1, TOP_K * TC), lambda i: (i, 0, 0), memory_space=pltpu.SMEM),
                  pl.BlockSpec((1, 1, TOP_K * TC), lambda i: (jnp.minimum(i + 1, nt - 1), 0, 0),
                               memory_space=pltpu.SMEM),
                  pl.BlockSpec(memory_space=pl.ANY),
                  pl.BlockSpec((TC, TOP_K), lambda i: (i, 0)),
                  pl.BlockSpec((TC, d), lambda i: (i, 0)),
                  pl.BlockSpec((1, d), lambda i: (0, 0)),
                  pl.BlockSpec((1, d), lambda i: (0, 0))],
        out_specs=[pl.BlockSpec((TC, d), lambda i: (i, 0)),
                   pl.BlockSpec((TC, d), lambda i: (i, 0))],
        out_shape=[jax.ShapeDtypeStruct((m, d), F32), jax.ShapeDtypeStruct((m, d), BF16)],
        scratch_shapes=[pltpu.VMEM((2, TOP_K * TC, d // 2), jnp.uint32), pltpu.SemaphoreType.DMA((2,))],
        compiler_params=_params(1, vmem + (8 << 20)),
        name="moe_combine_ln",
    )(pos, pos, y, gates, x, g, b)


def _routing_tables(eid, TM, TC):
    L = eid.shape[1]
    n_assign = TOP_K * L
    tmax = n_assign // TM + N_EXPERTS
    flat = eid.reshape(-1)
    onehot = (flat[:, None] == jnp.arange(N_EXPERTS, dtype=jnp.int32)[None, :]).astype(jnp.int32)
    csum = jnp.cumsum(onehot, axis=0)
    counts = csum[-1]
    rank = jnp.sum((csum - onehot) * onehot, axis=1)
    tiles_per = (counts + TM - 1) // TM
    tile_end = jnp.cumsum(tiles_per)
    tile_start = tile_end - tiles_per
    pos = tile_start[flat] * TM + rank
    token = jnp.tile(jnp.arange(L, dtype=jnp.int32), TOP_K)
    row_token = jnp.zeros((tmax * TM,), jnp.int32).at[pos].set(token)
    tile_ids = jnp.arange(tmax, dtype=jnp.int32)
    tile_expert = jnp.minimum(jnp.sum((tile_ids[:, None] >= tile_end[None, :]).astype(jnp.int32), axis=1),
                              N_EXPERTS - 1).astype(jnp.int32)
    n_tiles = tile_end[-1:].astype(jnp.int32)
    tile_expert = jnp.where(tile_ids < n_tiles[0], tile_expert, tile_expert[jnp.maximum(n_tiles[0] - 1, 0)])
    pos_tiles = pos.reshape(TOP_K, L // TC, TC).transpose(1, 0, 2).reshape(L // TC, 1, TOP_K * TC)
    return row_token.reshape(tmax, 1, TM), pos_tiles.astype(jnp.int32), tile_expert, n_tiles


def _moe_block(x, xp, eid, gates, w_gate, w_up, w_down, e0, ln_g, ln_b, alpha):
    L = x.shape[0]
    TM = min(256, L)
    TC = min(128, L)
    row_token, pos, tile_expert, n_tiles = _routing_tables(eid, TM, TC)
    g, xs = moe_gate(xp, w_gate, e0, tile_expert, n_tiles, row_token, TM)
    act = moe_up(xs, g, w_up, e0, tile_expert, n_tiles, TM)
    y = moe_down(act, w_down, e0, tile_expert, n_tiles, TM)
    return moe_combine_ln(y, pos, gates.T, x, ln_g, ln_b, alpha, TC)


def _ab_mixer(xb, w_in_all, w_out_all, li, sink, pe, w1, b1, w2, b2):
    L, d = xb.shape
    ha = d // (2 * HEAD_DIM)
    hb = d // (2 * HEAD_DIM)
    aq, akv, bq = ha * HEAD_DIM, A_KV_HEADS * HEAD_DIM, hb * HEAD_DIM
    bkv = N_BRANCH * 2 * HEAD_DIM
    ngate = hb * N_BRANCH
    s0, s2, s3, s4 = aq, aq + 2 * akv, aq + 2 * akv + bq, aq + 2 * akv + bq + bkv
    chunks = ((0, s0, Q_SCALE), (s2, bq, Q_SCALE), (s0, 2 * akv, 1.0), (s3, bkv, 1.0), (s4, ngate, 1.0))
    w = cast_weight(w_in_all, li, chunks, pad_to=512)
    proj = proj_matmul(xb, w, BF16)
    qa0, qb0 = 0, ha
    ka0 = ha + hb
    va0 = ka0 + A_KV_HEADS
    kvb0 = va0 + A_KV_HEADS
    gate_blk = kvb0 + 2 * N_BRANCH
    slopes_a = _alibi_slopes2(ha)
    slopes_b = _alibi_slopes2(hb)
    oa = banded_attention(proj, slopes_a, sink.astype(F32) * LOG2E, q_base=qa0, k_blk=ka0, v_blk=va0,
                          n_kv=A_KV_HEADS, G=ha // A_KV_HEADS, window=A_WINDOW, has_sink=True)
    ng = L // CMP_STRIDE
    n_cmp = (L - CMP_LEN) // CMP_STRIDE + 1
    n_sel = L // SEL_BLOCK
    nsp = -(-n_sel // LANES) * LANES
    groups = proj[kvb0:kvb0 + 2].reshape(2, ng, CMP_STRIDE * HEAD_DIM)
    kvc = nsa_compress(groups, pe.reshape(2, 1, CMP_LEN * HEAD_DIM).astype(BF16), w1.astype(BF16),
                       b1.reshape(2, 1, CMP_HIDDEN).astype(F32), w2.astype(BF16),
                       b2.reshape(2, 1, HEAD_DIM).astype(F32))
    cstart = np.arange(ng) * CMP_STRIDE
    sstart = np.arange(nsp) * SEL_BLOCK
    overlap = ((cstart[:, None] < sstart[None, :] + SEL_BLOCK) & (cstart[:, None] + CMP_LEN > sstart[None, :])
               & (np.arange(ng)[:, None] < n_cmp) & (np.arange(nsp)[None, :] < n_sel))
    overlap = jnp.asarray(overlap.astype(np.float32), dtype=BF16)
    ob, sel = nsa_cmp_attention(proj, kvc, slopes_b, overlap, q_base=qb0, H=hb, gate_blk=gate_blk,
                                n_cmp=n_cmp, n_sel=n_sel)
    kcols, qx, unsel = _sel_extra_columns(L, slopes_b, sel, n_sel)
    kx = jnp.concatenate([proj[kvb0 + 2], kcols], axis=1)
    vx = jnp.concatenate([proj[kvb0 + 3], jnp.ones((L, LANES), BF16)], axis=1)
    ob = nsa_sel_attention(proj, kx, vx, qx, unsel, ob, q_base=qb0, H=hb, gate_blk=gate_blk)
    ob = banded_attention(proj, slopes_b, jnp.zeros((hb,), F32), q_base=qb0, k_blk=kvb0 + 4, v_blk=kvb0 + 5,
                          n_kv=1, G=hb, window=B_WINDOW, gate_blk=gate_blk, gate_col=2, oacc=ob)
    return outproj([oa, ob], cast_weight(w_out_all, li))


def _c_mixer(xb, w_in_all, w_out_all, li):
    d = xb.shape[1]
    hc = d // HEAD_DIM
    qkv = proj_matmul(xb, w_in_all, F32, layer=li, n_q_cols=d)
    o = dilated_attention(qkv, _alibi_slopes2(hc), H=hc)
    return outproj([o], cast_weight(w_out_all, li))


def kernel(x, ab_w_in, ab_w_out, a_sink, nsa_cmp_pe, nsa_cmp_w1, nsa_cmp_b1, nsa_cmp_w2, nsa_cmp_b2, c_w_in, c_w_out, ln_mix_g, ln_mix_b, ln_ffn_g, ln_ffn_b, router_w, router_b, moe_w_gate, moe_w_up, moe_w_down):
    batch, L, d = x.shape
    depth = ln_mix_g.shape[0]
    alpha = float((2 * depth) ** 0.25)
    router_wt = router_w.T.astype(F32)
    router_bc = router_b.reshape(N_EXPERTS, 1).astype(F32)
    d_ff = moe_w_gate.shape[-1]
    wg_all = moe_w_gate.astype(F32).reshape(depth * N_EXPERTS, d, d_ff)
    wu_all = moe_w_up.astype(F32).reshape(depth * N_EXPERTS, d, d_ff)
    wd_all = moe_w_down.astype(F32).reshape(depth * N_EXPERTS, d_ff, d)
    outs = []
    for bi in range(batch):
        xf = x[bi].astype(F32)
        xb = xf.astype(BF16)
        for layer in range(depth):
            i = layer // 2
            g_mix, b_mix = ln_mix_g[layer].reshape(1, d), ln_mix_b[layer].reshape(1, d)
            g_ffn, b_ffn = ln_ffn_g[layer].reshape(1, d), ln_ffn_b[layer].reshape(1, d)
            if layer % 2 == 0:
                mix = _ab_mixer(xb, ab_w_in, ab_w_out, i, a_sink[i], nsa_cmp_pe[i], nsa_cmp_w1[i],
                                nsa_cmp_b1[i], nsa_cmp_w2[i], nsa_cmp_b2[i])
            else:
                mix = _c_mixer(xb, c_w_in, c_w_out, i)
            xf, xp, eid, gates = ln_router(mix, xf, g_mix, b_mix, router_wt, router_bc, alpha)
            xf, xb = _moe_block(xf, xp, eid, gates, wg_all, wu_all, wd_all, layer * N_EXPERTS, g_ffn, b_ffn,
                                alpha)
        outs.append(xf)
    return jnp.stack(outs, axis=0).astype(x.dtype)
```
